```python
import jax, jax.numpy as jnp
from jax import lax
import numpy as np

D_MODEL = 1024
BATCH = 8
SEQ = 4096
DEPTH = 4
DEC_BATCH = 32
DEC_SEQ = 16
PAST_LEN = 2048

CHUNK = 64
N_META = 16
A_HEADS = 8
A_HEAD_DIM = 64
A_WIDTH = A_HEADS * A_HEAD_DIM
DECAY_LORA = 64
ICLR_LORA = 64
B_HEADS = 8
QK_NOPE = 64
QK_ROPE = 32
QK_DIM = QK_NOPE + QK_ROPE
V_DIM = 64
B_WIDTH = B_HEADS * V_DIM
Q_LORA = 384
KV_LORA = 256
ROPE_THETA = 10000.0
Q_BLOCK = 128
NORM_EPS = 1e-6
GN_EPS = 64e-5
NEG_INF = -1e30
A_COLS = 4 * A_WIDTH + DECAY_LORA + ICLR_LORA
B_COLS = Q_LORA + KV_LORA + QK_ROPE + B_WIDTH
G_COLS = 2 * D_MODEL
IN_COLS = A_COLS + B_COLS + G_COLS

kernel_name = 'rwkv7_mla_parallel_streaming_encoder'


def rmsnorm(x, g, eps=NORM_EPS):
    xf = x.astype(jnp.float32)
    y = xf * lax.rsqrt(jnp.mean(xf * xf, axis=-1, keepdims=True) + eps)
    return (y * g.astype(jnp.float32)).astype(x.dtype)


def rope(x, t):
    inv = ROPE_THETA ** (-jnp.arange(0, QK_ROPE, 2, dtype=jnp.float32) / QK_ROPE)
    ang = t.astype(jnp.float32)[:, None] * inv[None, :]
    if x.ndim == 4:
        ang = ang[:, None, :]
    cos = jnp.cos(ang).astype(x.dtype)
    sin = jnp.sin(ang).astype(x.dtype)
    x1, x2 = x[..., :QK_ROPE // 2], x[..., QK_ROPE // 2:]
    return jnp.concatenate([x1 * cos - x2 * sin, x1 * sin + x2 * cos], axis=-1)


def rwkv_scan(s0, r, decay, k, v, kk, a):
    def step(S, inp):
        rt, wt, kt, vt, kkt, at = inp
        sa = jnp.einsum('bhvk,bhk->bhv', S, -kkt)
        S = S * wt[:, :, None, :] + sa[..., None] * (kkt * at)[:, :, None, :] + vt[..., None] * kt[:, :, None, :]
        return S, jnp.einsum('bhvk,bhk->bhv', S, rt)
    xs = (jnp.moveaxis(r, 1, 0), jnp.moveaxis(decay, 1, 0), jnp.moveaxis(k, 1, 0),
          jnp.moveaxis(v, 1, 0), jnp.moveaxis(kk, 1, 0), jnp.moveaxis(a, 1, 0))
    s_fin, ys = lax.scan(step, s0, xs)
    return jnp.moveaxis(ys, 0, 1), s_fin


def rwkv_branch(a_cols, shift_row, s0, p):
    B, L, _ = a_cols.shape
    f32 = jnp.float32
    prev = jnp.concatenate([shift_row.astype(a_cols.dtype), a_cols[:, :-1]], axis=1)
    xs = a_cols * p['shift_mix'][0] + prev * p['shift_mix'][1]
    r, k, v, gate, wl, al = jnp.split(
        xs, [A_WIDTH, 2 * A_WIDTH, 3 * A_WIDTH, 4 * A_WIDTH, 4 * A_WIDTH + DECAY_LORA], axis=-1)
    w = -jax.nn.softplus(-(p['rwkv_w0'] + jnp.tanh(wl) @ p['rwkv_w2']).astype(f32)) - 0.5
    decay = jnp.exp(-jnp.exp(w))
    a = jax.nn.sigmoid((p['rwkv_a0'] + al @ p['rwkv_a2']).astype(f32))
    hd = lambda u: u.astype(f32).reshape(B, L, A_HEADS, A_HEAD_DIM)
    kk = hd(k * p['rwkv_k_k'])
    kk = kk * lax.rsqrt(jnp.maximum(jnp.sum(kk * kk, axis=-1, keepdims=True), 1e-24))
    k_mod = k.astype(f32) * (1.0 + (a - 1.0) * p['rwkv_k_a'].astype(f32))
    rh, kh, vh, ah, dh = hd(r), hd(k_mod), hd(v), hd(a), hd(decay)
    y, s_fin = rwkv_scan(s0.astype(f32), rh, dh, kh, vh, kk, ah)
    mu = jnp.mean(y, axis=-1, keepdims=True)
    var = jnp.mean(jnp.square(y - mu), axis=-1, keepdims=True)
    yn = ((y - mu) * lax.rsqrt(var + GN_EPS)).reshape(B, L, A_WIDTH) * p['rwkv_ln_w'] + p['rwkv_ln_b']
    bonus = (jnp.sum(rh * kh * p['rwkv_r_k'].astype(f32), axis=-1, keepdims=True) * vh).reshape(B, L, A_WIDTH)
    out = ((yn + bonus) * jax.nn.silu(gate.astype(f32))).astype(a_cols.dtype)
    return out, s_fin.astype(s0.dtype)


def mla_project(b_cols, t, p):
    B, L, _ = b_cols.shape
    qc, ckv, kr, gate = jnp.split(b_cols, [Q_LORA, Q_LORA + KV_LORA, Q_LORA + KV_LORA + QK_ROPE], axis=-1)
    q = (rmsnorm(qc, p['mla_q_norm']) @ p['mla_w_uq']).reshape(B, L, B_HEADS, QK_DIM)
    q = jnp.concatenate([rmsnorm(q[..., :QK_NOPE], p['mla_qn_nope']),
                         rope(rmsnorm(q[..., QK_NOPE:], p['mla_qn_rope']), t)], axis=-1)
    latent = rmsnorm(ckv, p['mla_kv_norm'])
    krope = rope(rmsnorm(kr, p['mla_kn_rope']), t)
    return q, latent, krope, gate


def mla_expand(latent, krope, p):
    B, L, _ = latent.shape
    kv = (latent @ p['mla_w_ukv']).reshape(B, L, B_HEADS, QK_NOPE + V_DIM)
    k_nope = rmsnorm(kv[..., :QK_NOPE], p['mla_kn_nope'])
    k = jnp.concatenate([k_nope, jnp.broadcast_to(krope[:, :, None, :], (B, L, B_HEADS, QK_ROPE))], axis=-1)
    return k, kv[..., QK_NOPE:]


def attn_chunk_causal(q, k, v, chunk_id):
    B, L = q.shape[:2]
    nb = -(-L // Q_BLOCK)
    Lp = nb * Q_BLOCK
    qp = jnp.pad(q, ((0, 0), (0, Lp - L), (0, 0), (0, 0)))
    qcid = jnp.pad(chunk_id, (0, Lp - L), mode='edge')
    qb = jnp.moveaxis(qp.reshape(B, nb, Q_BLOCK, B_HEADS, QK_DIM), 1, 0)
    cb = qcid.reshape(nb, Q_BLOCK)
    scale = QK_DIM ** -0.5

    def block(args):
        qi, ci = args
        s = jnp.einsum('bqhd,bkhd->bhqk', qi, k).astype(jnp.float32) * scale
        mask = chunk_id[None, :] <= ci[:, None]
        s = jnp.where(mask[None, None], s, NEG_INF)
        pr = jax.nn.softmax(s, axis=-1).astype(v.dtype)
        return jnp.einsum('bhqk,bkhd->bqhd', pr, v)

    o = lax.map(block, (qb, cb))
    return jnp.moveaxis(o, 0, 1).reshape(B, Lp, B_HEADS, V_DIM)[:, :L]


def attn_full(q, k, v):
    s = jnp.einsum('bqhd,bkhd->bhqk', q, k).astype(jnp.float32) * (QK_DIM ** -0.5)
    pr = jax.nn.softmax(s, axis=-1).astype(v.dtype)
    return jnp.einsum('bhqk,bkhd->bqhd', pr, v)


def mixer_layer(x, p, t, shift_row, s0, past_latent=None, past_krope=None, chunk_id=None):
    B, L, _ = x.shape
    h = rmsnorm(x, p['norm_w'])
    z = h @ p['w_in']
    a_cols, b_cols, g_cols = jnp.split(z, [A_COLS, A_COLS + B_COLS], axis=-1)
    ya, s_fin = rwkv_branch(a_cols, shift_row, s0, p)
    q, latent, krope, gate_b = mla_project(b_cols, t, p)
    if past_latent is None:
        k, v = mla_expand(latent, krope, p)
        o = attn_chunk_causal(q, k, v, chunk_id)
    else:
        k, v = mla_expand(jnp.concatenate([past_latent, latent], axis=1),
                          jnp.concatenate([past_krope, krope], axis=1), p)
        o = attn_full(q, k, v)
    yb = o.reshape(B, L, B_WIDTH) * jax.nn.silu(gate_b)
    ga, gb = jnp.split(jax.nn.sigmoid(g_cols), 2, axis=-1)
    y = (ga * (ya @ p['w_branch_a']) + gb * (yb @ p['w_branch_b'])) @ p['w_out']
    return x + y, s_fin, a_cols[:, -1:], latent, krope


def setup_inputs(seed: int = 0) -> dict:
    key = jax.random.key(seed)
    ks = jax.random.split(key, 32)
    f32 = jnp.float32
    nrm = lambda k, shape, s: jax.random.normal(k, shape, f32) * s
    mu = jax.random.uniform(ks[5], (DEPTH, A_COLS), f32)
    return {
        'x_prompt': nrm(ks[0], (BATCH, SEQ, D_MODEL), 1.0),
        'x_sample': nrm(ks[1], (DEC_BATCH, DEC_SEQ, D_MODEL), 1.0),
        'state_rwkv': nrm(ks[2], (DEPTH, DEC_BATCH, A_HEADS, A_HEAD_DIM, A_HEAD_DIM), 0.3),
        'state_shift': nrm(ks[3], (DEPTH, DEC_BATCH, 1, A_COLS), 1.0),
        'cache_mla_latent': nrm(ks[4], (DEPTH, DEC_BATCH, PAST_LEN, KV_LORA), 1.0),
        'cache_mla_krope': nrm(ks[6], (DEPTH, DEC_BATCH, PAST_LEN, QK_ROPE), 1.0),
        'meta_tokens': nrm(ks[7], (N_META, D_MODEL), 1.0),
        'norm_w': 1.0 + nrm(ks[8], (DEPTH, D_MODEL), 0.02),
        'w_in': nrm(ks[9], (DEPTH, D_MODEL, IN_COLS), D_MODEL ** -0.5),
        'shift_mix': jnp.stack([1.0 - mu, mu], axis=1),
        'rwkv_w0': jax.random.uniform(ks[10], (DEPTH, A_WIDTH), f32, -6.0, 0.0),
        'rwkv_w2': nrm(ks[11], (DEPTH, DECAY_LORA, A_WIDTH), 0.5 * DECAY_LORA ** -0.5),
        'rwkv_a0': nrm(ks[12], (DEPTH, A_WIDTH), 0.1),
        'rwkv_a2': nrm(ks[13], (DEPTH, ICLR_LORA, A_WIDTH), 0.5 * ICLR_LORA ** -0.5),
        'rwkv_k_k': 0.85 + nrm(ks[14], (DEPTH, A_WIDTH), 0.05),
        'rwkv_k_a': 1.0 + nrm(ks[15], (DEPTH, A_WIDTH), 0.05),
        'rwkv_r_k': nrm(ks[16], (DEPTH, A_HEADS, A_HEAD_DIM), 0.1),
        'rwkv_ln_w': 1.0 + nrm(ks[17], (DEPTH, A_WIDTH), 0.02),
        'rwkv_ln_b': nrm(ks[18], (DEPTH, A_WIDTH), 0.02),
        'mla_q_norm': 1.0 + nrm(ks[19], (DEPTH, Q_LORA), 0.02),
        'mla_w_uq': nrm(ks[20], (DEPTH, Q_LORA, B_HEADS * QK_DIM), Q_LORA ** -0.5),
        'mla_kv_norm': 1.0 + nrm(ks[21], (DEPTH, KV_LORA), 0.02),
        'mla_w_ukv': nrm(ks[22], (DEPTH, KV_LORA, B_HEADS * (QK_NOPE + V_DIM)), KV_LORA ** -0.5),
        'mla_qn_nope': 1.0 + nrm(ks[23], (DEPTH, QK_NOPE), 0.02),
        'mla_kn_nope': 1.0 + nrm(ks[24], (DEPTH, QK_NOPE), 0.02),
        'mla_qn_rope': 1.0 + nrm(ks[25], (DEPTH, QK_ROPE), 0.02),
        'mla_kn_rope': 1.0 + nrm(ks[26], (DEPTH, QK_ROPE), 0.02),
        'w_branch_a': nrm(ks[27], (DEPTH, A_WIDTH, D_MODEL), A_WIDTH ** -0.5),
        'w_branch_b': nrm(ks[28], (DEPTH, B_WIDTH, D_MODEL), B_WIDTH ** -0.5),
        'w_out': nrm(ks[29], (DEPTH, D_MODEL, D_MODEL), D_MODEL ** -0.5),
    }


def reference(x_prompt, x_sample, state_rwkv, state_shift, cache_mla_latent, cache_mla_krope,
              meta_tokens, norm_w, w_in, shift_mix, rwkv_w0, rwkv_w2, rwkv_a0, rwkv_a2,
              rwkv_k_k, rwkv_k_a, rwkv_r_k, rwkv_ln_w, rwkv_ln_b, mla_q_norm, mla_w_uq,
              mla_kv_norm, mla_w_ukv, mla_qn_nope, mla_kn_nope, mla_qn_rope, mla_kn_rope,
              w_branch_a, w_branch_b, w_out):
    bp, seq = x_prompt.shape[:2]
    past = cache_mla_latent.shape[2]
    meta = jnp.broadcast_to(meta_tokens[None].astype(x_prompt.dtype), (bp, N_META, D_MODEL))
    xp = jnp.concatenate([meta, x_prompt], axis=1)
    t_p = jnp.arange(-N_META, seq, dtype=jnp.int32)
    cid = jnp.where(t_p < 0, -1, t_p // CHUNK)
    xs = x_sample
    t_s = past + jnp.arange(x_sample.shape[1], dtype=jnp.int32)
    sp_list, shp_list, latp_list, krp_list = [], [], [], []
    ss_list, shs_list, lats_list, krs_list = [], [], [], []
    for l in range(DEPTH):
        p = {'norm_w': norm_w[l], 'w_in': w_in[l], 'shift_mix': shift_mix[l],
             'rwkv_w0': rwkv_w0[l], 'rwkv_w2': rwkv_w2[l], 'rwkv_a0': rwkv_a0[l], 'rwkv_a2': rwkv_a2[l],
             'rwkv_k_k': rwkv_k_k[l], 'rwkv_k_a': rwkv_k_a[l], 'rwkv_r_k': rwkv_r_k[l],
             'rwkv_ln_w': rwkv_ln_w[l], 'rwkv_ln_b': rwkv_ln_b[l],
             'mla_q_norm': mla_q_norm[l], 'mla_w_uq': mla_w_uq[l], 'mla_kv_norm': mla_kv_norm[l],
             'mla_w_ukv': mla_w_ukv[l], 'mla_qn_nope': mla_qn_nope[l], 'mla_kn_nope': mla_kn_nope[l],
             'mla_qn_rope': mla_qn_rope[l], 'mla_kn_rope': mla_kn_rope[l],
             'w_branch_a': w_branch_a[l], 'w_branch_b': w_branch_b[l], 'w_out': w_out[l]}
        zero_row = jnp.zeros((bp, 1, A_COLS), xp.dtype)
        zero_state = jnp.zeros((bp, A_HEADS, A_HEAD_DIM, A_HEAD_DIM), jnp.float32)
        xp, s_p, sh_p, lat_p, kr_p = mixer_layer(xp, p, t_p, zero_row, zero_state, chunk_id=cid)
        xs, s_s, sh_s, lat_s, kr_s = mixer_layer(xs, p, t_s, state_shift[l], state_rwkv[l],
                                                 cache_mla_latent[l], cache_mla_krope[l])
        sp_list.append(s_p); shp_list.append(sh_p); latp_list.append(lat_p); krp_list.append(kr_p)
        ss_list.append(s_s); shs_list.append(sh_s); lats_list.append(lat_s); krs_list.append(kr_s)
    y_prompt = xp[:, N_META:]
    y_sample = xs
    state_rwkv_prompt = jnp.stack(sp_list)
    state_shift_prompt = jnp.stack(shp_list)
    cache_mla_latent_prompt = jnp.stack(latp_list)
    cache_mla_krope_prompt = jnp.stack(krp_list)
    state_rwkv_sample = jnp.stack(ss_list)
    state_shift_sample = jnp.stack(shs_list)
    cache_mla_latent_new = jnp.stack(lats_list)
    cache_mla_krope_new = jnp.stack(krs_list)
    return (y_prompt, y_sample, state_rwkv_prompt, state_shift_prompt, cache_mla_latent_prompt,
            cache_mla_krope_prompt, state_rwkv_sample, state_shift_sample, cache_mla_latent_new,
            cache_mla_krope_new)
```

```python
import functools

import numpy as np
import jax
import jax.numpy as jnp
from jax import lax
from jax.experimental import pallas as pl
from jax.experimental.pallas import tpu as pltpu

F32 = jnp.float32
BF16 = jnp.bfloat16

D_MODEL = 1024
CHUNK = 64
N_META = 16
A_HEADS = 8
A_HEAD_DIM = 64
A_WIDTH = A_HEADS * A_HEAD_DIM
DECAY_LORA = 64
ICLR_LORA = 64
B_HEADS = 8
QK_NOPE = 64
QK_ROPE = 32
QK_DIM = QK_NOPE + QK_ROPE
V_DIM = 64
B_WIDTH = B_HEADS * V_DIM
Q_LORA = 384
KV_LORA = 256
ROPE_THETA = 10000.0
NORM_EPS = 1e-6
GN_EPS = 64e-5
NEG_INF = -1e30
A_COLS = 4 * A_WIDTH + DECAY_LORA + ICLR_LORA
B_COLS = Q_LORA + KV_LORA + QK_ROPE + B_WIDTH
G_COLS = 2 * D_MODEL

LANES = 128
HEAD_PAD = LANES
QK_WIDTH = B_HEADS * HEAD_PAD
ZQ_COLS = Q_LORA + KV_LORA + LANES
ROPE_LO = QK_NOPE
PAD_FRONT = 128
VMEM_LIMIT = 56 * 1024 * 1024
BASE_BLOCK = 8


def _cparams(sem):
    return pltpu.CompilerParams(dimension_semantics=sem, vmem_limit_bytes=VMEM_LIMIT)


def _mm(a, b):
    return jnp.dot(a.astype(BF16), b.astype(BF16), preferred_element_type=F32)


def _mm_nt(a, b):
    return lax.dot_general(a.astype(BF16), b.astype(BF16), (((1,), (1,)), ((), ())),
                           preferred_element_type=F32)


def _mm_tn(a, b):
    return lax.dot_general(a.astype(BF16), b.astype(BF16), (((0,), (0,)), ((), ())),
                           preferred_element_type=F32)


def _split3(x):
    hi = x.astype(BF16)
    r1 = x - hi.astype(F32)
    mid = r1.astype(BF16)
    lo = (r1 - mid.astype(F32)).astype(BF16)
    return hi, mid, lo


def _mm_x_exact(x, e):
    hi, mid, lo = _split3(x)
    d = lambda u: jnp.dot(u, e, preferred_element_type=F32)
    return d(hi) + d(mid) + d(lo)


def _mm_exact_x(e, x):
    hi, mid, lo = _split3(x)
    d = lambda u: jnp.dot(e, u, preferred_element_type=F32)
    return d(hi) + d(mid) + d(lo)


def _sigmoid(x):
    return 1.0 / (1.0 + jnp.exp(-x))


def _in_proj_kernel(x_ref, nw_ref, w_ref, za_ref, zq_ref, zgb_ref, zg_ref, *, tm, n_pad):
    x = x_ref[0]
    ms = jnp.mean(x * x, axis=-1, keepdims=True)
    h = x * lax.rsqrt(ms + NORM_EPS) * nw_ref[0]
    if n_pad:
        row = pl.program_id(1) * tm + lax.broadcasted_iota(jnp.int32, (tm, 1), 0)
        h = jnp.where(row >= n_pad, h, 0.0)
    hb = h.astype(BF16)
    o = 0
    for ref, width in ((za_ref, A_COLS), (zq_ref, ZQ_COLS), (zgb_ref, B_WIDTH), (zg_ref, G_COLS)):
        ref[0] = jnp.dot(hb, w_ref[0, :, o:o + width], preferred_element_type=F32)
        o += width


def _in_proj(x, norm_w, w_in_p, layer, tm, n_pad):
    bv, lv, _ = x.shape
    cols = w_in_p.shape[-1]
    row = lambda w: pl.BlockSpec((1, tm, w), lambda b, j: (b, j, 0))
    return pl.pallas_call(
        functools.partial(_in_proj_kernel, tm=tm, n_pad=n_pad),
        grid=(bv, lv // tm),
        in_specs=[row(D_MODEL),
                  pl.BlockSpec((1, 1, D_MODEL), lambda b, j: (layer, 0, 0)),
                  pl.BlockSpec((1, D_MODEL, cols), lambda b, j: (layer, 0, 0))],
        out_specs=[row(A_COLS), row(ZQ_COLS), row(B_WIDTH), row(G_COLS)],
        out_shape=[jax.ShapeDtypeStruct((bv, lv, w), F32) for w in (A_COLS, ZQ_COLS, B_WIDTH, G_COLS)],
        compiler_params=_cparams(("parallel", "parallel")),
        name="in_proj",
    )(x, norm_w, w_in_p)


def _rwkv_prep_kernel(za_ref, zap_ref, sh_ref, mix_ref, w0_ref, a0_ref, lora_ref, kk_ref, ka_ref,
                      rk_ref, e_ref, tri_ref, same_ref,
                      at_ref, rt_ref, bt_ref, kt_ref, vb_ref, pc_ref, bonus_ref, sg_ref, *, tm):
    j = pl.program_id(1)
    a = za_ref[0]
    first = jnp.where(j == 0, sh_ref[0], zap_ref[0, 7:8, :])
    rowid = lax.broadcasted_iota(jnp.int32, (tm, 1), 0)
    prev = jnp.where(rowid == 0, first, pltpu.roll(a, 1, 0))
    xs = a * mix_ref[0, 0:1, :] + prev * mix_ref[0, 1:2, :]
    r = xs[:, 0:A_WIDTH]
    k = xs[:, A_WIDTH:2 * A_WIDTH]
    v = xs[:, 2 * A_WIDTH:3 * A_WIDTH]
    gate = xs[:, 3 * A_WIDTH:4 * A_WIDTH]
    lr = xs[:, 4 * A_WIDTH:A_COLS]
    lane = lax.broadcasted_iota(jnp.int32, (1, DECAY_LORA + ICLR_LORA), 1)
    lr = jnp.where(lane < DECAY_LORA, jnp.tanh(lr), lr)
    lo = _mm(lr, lora_ref[0])
    wlin = w0_ref[0] + lo[:, 0:A_WIDTH]
    alin = a0_ref[0] + lo[:, A_WIDTH:2 * A_WIDTH]
    softplus_neg = jnp.maximum(-wlin, 0.0) + jnp.log1p(jnp.exp(-jnp.abs(wlin)))
    logdec = -jnp.exp(-softplus_neg - 0.5)
    av = _sigmoid(alin)
    e = e_ref[...]
    kk = k * kk_ref[0]
    kk = kk * lax.rsqrt(jnp.maximum(_mm_x_exact(kk * kk, e), 1e-24))
    kmod = k * (1.0 + (av - 1.0) * ka_ref[0])
    cum = _mm_exact_x(tri_ref[...], logdec)
    tot = _mm_exact_x(same_ref[...], logdec)
    pin = jnp.exp(cum)
    pex = jnp.exp(cum - logdec)
    pinv = jnp.exp(-cum)
    at_ref[0] = (-kk * pex).astype(BF16)
    rt_ref[0] = (r * pin).astype(BF16)
    bt_ref[0] = (kk * av * pinv).astype(BF16)
    kt_ref[0] = (kmod * pinv).astype(BF16)
    vb_ref[0] = v.astype(BF16)
    pc_ref[0] = jnp.exp(tot)
    bonus_ref[0] = _mm_x_exact(r * kmod * rk_ref[0], e) * v
    sg_ref[0] = gate * _sigmoid(gate)


def _rwkv_prep(za, shift_row, wts, layer, tm, chunk):
    bs, ls, _ = za.shape
    ids = np.arange(tm)
    same = (ids[:, None] // chunk) == (ids[None, :] // chunk)
    tri = same & (ids[None, :] <= ids[:, None])
    e = (np.arange(A_WIDTH)[:, None] // A_HEAD_DIM) == (np.arange(A_WIDTH)[None, :] // A_HEAD_DIM)
    row = lambda w: pl.BlockSpec((1, tm, w), lambda b, j: (b, j, 0))
    par = lambda r, c: pl.BlockSpec((1, r, c), lambda b, j: (layer, 0, 0))
    const = lambda r, c: pl.BlockSpec((r, c), lambda b, j: (0, 0))
    nsub = tm // 8
    outs = [jax.ShapeDtypeStruct((bs, ls, A_WIDTH), BF16)] * 5 + [jax.ShapeDtypeStruct((bs, ls, A_WIDTH), F32)] * 3
    return pl.pallas_call(
        functools.partial(_rwkv_prep_kernel, tm=tm),
        grid=(bs, ls // tm),
        in_specs=[row(A_COLS),
                  pl.BlockSpec((1, 8, A_COLS), lambda b, j: (b, jnp.maximum(j * nsub - 1, 0), 0)),
                  pl.BlockSpec((1, 1, A_COLS), lambda b, j: (b, 0, 0)),
                  par(2, A_COLS), par(1, A_WIDTH), par(1, A_WIDTH),
                  par(DECAY_LORA + ICLR_LORA, 2 * A_WIDTH),
                  par(1, A_WIDTH), par(1, A_WIDTH), par(1, A_WIDTH),
                  const(A_WIDTH, A_WIDTH), const(tm, tm), const(tm, tm)],
        out_specs=[row(A_WIDTH)] * 8,
        out_shape=outs,
        compiler_params=_cparams(("parallel", "parallel")),
        name="rwkv_prep",
    )(za, za, shift_row, wts["shift_mix"], wts["w0"], wts["a0"], wts["lora"], wts["k_k"], wts["k_a"],
      wts["r_k"], jnp.asarray(e, BF16), jnp.asarray(tri, BF16), jnp.asarray(same, BF16))


def _chain_masks(chunk, g):
    gc, w = g * chunk, g * A_HEAD_DIM
    t = np.arange(chunk)[:, None]
    s = np.arange(gc)[None, :] % chunk
    wide = {"strict": s < t, "incl": s <= t, "eye": s == t,
            "base": (s // BASE_BLOCK == t // BASE_BLOCK) & (s < t)}
    levels = []
    sz = BASE_BLOCK
    while sz < chunk:
        levels.append((s // (2 * sz) == t // (2 * sz)) & (s // sz != t // sz) & (s < t))
        sz *= 2
    rows = np.arange(gc)[:, None] // chunk
    bd_nat = rows == (np.arange(w)[None, :] // A_HEAD_DIM)
    bd_wide = rows == (np.arange(gc)[None, :] // chunk)
    bd_state = (np.arange(w)[:, None] // A_HEAD_DIM) == (np.arange(w)[None, :] // A_HEAD_DIM)
    f = lambda m: jnp.asarray(m, F32)
    b = lambda m: jnp.asarray(m, BF16)
    lev = np.stack(levels) if levels else np.zeros((1, chunk, gc), bool)
    return dict(strict=f(wide["strict"]), incl=f(wide["incl"]), eye=f(wide["eye"]), base=f(wide["base"]),
                lev=f(lev), bd_nat=b(bd_nat), bd_wide=b(bd_wide), bd_state=f(bd_state)), len(levels)


def _rwkv_chain_kernel(at_ref, rt_ref, bt_ref, kt_ref, vb_ref, pc_ref, bonus_ref, sg_ref, s0_ref,
                       lnw_ref, lnb_ref, e_ref, strict_ref, incl_ref, eye_ref, base_ref, lev_ref,
                       bdn_ref, bdw_ref, bds_ref, ya_ref, s_ref, *, nb, chunk, g, n_lev):
    c = chunk
    gc, w = g * c, g * A_HEAD_DIM
    n_groups = A_HEADS // g

    @pl.when(pl.program_id(1) == 0)
    def _():
        s_ref[...] = s0_ref[...]

    strict, incl, eye, base = strict_ref[...], incl_ref[...], eye_ref[...], base_ref[...]
    bdn, bdw, bds = bdn_ref[...], bdw_ref[...], bds_ref[...]

    def bd(x, mask):
        return jnp.concatenate([x.astype(BF16)] * g, axis=0) * mask

    def wide_mm(x, y):
        return jnp.dot(x.astype(BF16), bd(y, bdw), preferred_element_type=F32)

    ys = []
    for b in range(nb):
        yg = []
        for grp in range(n_groups):
            sl = slice(grp * w, (grp + 1) * w)
            at, rt, bt, kt, vb = (ref[b, :, sl] for ref in (at_ref, rt_ref, bt_ref, kt_ref, vb_ref))
            ar = jnp.concatenate([at, rt], axis=0)
            vbd = bd(vb, bdn)
            gm = _mm_nt(ar, jnp.concatenate([bd(bt, bdn), bd(kt, bdn)], axis=0))
            m_ab = gm[:c, :gc] * strict
            m_ak = gm[:c, gc:] * strict
            n_rb = gm[c:, :gc] * incl
            n_rk = gm[c:, gc:] * incl
            s_old = s_ref[b, grp]
            ah = _mm_nt(ar, s_old)
            rhs = ah[:c] + jnp.dot(m_ak.astype(BF16), vbd, preferred_element_type=F32)
            d1 = m_ab * base
            d2 = wide_mm(d1, d1)
            t = eye + d1
            t = t + wide_mm(t, d2)
            t = t + wide_mm(t, wide_mm(d2, d2))
            for lv in range(n_lev):
                t = t + wide_mm(t, wide_mm(m_ab * lev_ref[lv], t))
            u = jnp.dot(t.astype(BF16), bd(rhs, bdn), preferred_element_type=F32)
            ub = u.astype(BF16)
            y = ah[c:] + jnp.dot(jnp.concatenate([n_rb, n_rk], axis=1).astype(BF16),
                                 jnp.concatenate([bd(ub, bdn), vbd], axis=0),
                                 preferred_element_type=F32)
            upd = _mm_tn(jnp.concatenate([ub, vb], axis=0), jnp.concatenate([bt, kt], axis=0))
            s_ref[b, grp] = (s_old + upd * bds) * pc_ref[b, 0:1, sl]
            yg.append(y)
        ys.append(yg[0] if n_groups == 1 else jnp.concatenate(yg, axis=1))
    y = ys[0] if nb == 1 else jnp.concatenate(ys, axis=0)
    e = e_ref[...]
    inv_n = 1.0 / A_HEAD_DIM
    mu = _mm_x_exact(y, e) * inv_n
    d = y - mu
    var = _mm_x_exact(d * d, e) * inv_n
    yn = d * lax.rsqrt(var + GN_EPS) * lnw_ref[0] + lnb_ref[0]
    bonus = bonus_ref[...].reshape(nb * c, A_WIDTH)
    sg = sg_ref[...].reshape(nb * c, A_WIDTH)
    ya_ref[...] = ((yn + bonus) * sg).astype(BF16).reshape(nb, c, A_WIDTH)


def _rwkv_chain(prep, s0_bd, wts, layer, nb, chunk, g):
    at, rt, bt, kt, vb, pc, bonus, sg = prep
    bs, ls, _ = at.shape
    gc, w = g * chunk, g * A_HEAD_DIM
    n_groups = A_HEADS // g
    masks, n_lev = _chain_masks(chunk, g)
    e = (np.arange(A_WIDTH)[:, None] // A_HEAD_DIM) == (np.arange(A_WIDTH)[None, :] // A_HEAD_DIM)
    row = pl.BlockSpec((nb, chunk, A_WIDTH), lambda b, j: (b, j, 0))
    par = pl.BlockSpec((1, 1, A_WIDTH), lambda b, j: (layer, 0, 0))
    state = pl.BlockSpec((nb, n_groups, w, w), lambda b, j: (b, 0, 0, 0))
    const = lambda shape: pl.BlockSpec(shape, lambda b, j: (0,) * len(shape))
    return pl.pallas_call(
        functools.partial(_rwkv_chain_kernel, nb=nb, chunk=chunk, g=g, n_lev=n_lev),
        grid=(bs // nb, ls // chunk),
        in_specs=[row] * 5 + [pl.BlockSpec((nb, 8, A_WIDTH), lambda b, j: (b, j * (chunk // 8), 0)),
                              row, row, state, par, par, const((A_WIDTH, A_WIDTH)),
                              const((chunk, gc)), const((chunk, gc)), const((chunk, gc)), const((chunk, gc)),
                              const(tuple(masks["lev"].shape)),
                              const((gc, w)), const((gc, gc)), const((w, w))],
        out_specs=[row, state],
        out_shape=[jax.ShapeDtypeStruct((bs, ls, A_WIDTH), BF16),
                   jax.ShapeDtypeStruct((bs, n_groups, w, w), F32)],
        compiler_params=_cparams(("parallel", "arbitrary")),
        name="rwkv_chain",
    )(at, rt, bt, kt, vb, pc, bonus, sg, s0_bd, wts["ln_w"], wts["ln_b"], jnp.asarray(e, BF16),
      masks["strict"], masks["incl"], masks["eye"], masks["base"], masks["lev"],
      masks["bd_nat"], masks["bd_wide"], masks["bd_state"])


def _state_to_bd(s, g):
    b = s.shape[0]
    n_groups = A_HEADS // g
    s = s.reshape(b, n_groups, g, A_HEAD_DIM, A_HEAD_DIM)
    eye = jnp.eye(g, dtype=s.dtype)
    out = jnp.einsum("bnhvk,hi->bnhvik", s, eye)
    return out.reshape(b, n_groups, g * A_HEAD_DIM, g * A_HEAD_DIM)


def _state_from_bd(s_bd, g):
    b, n_groups = s_bd.shape[:2]
    s = s_bd.reshape(b, n_groups, g, A_HEAD_DIM, g, A_HEAD_DIM)
    s = jnp.stack([s[:, :, h, :, h, :] for h in range(g)], axis=2)
    return s.reshape(b, A_HEADS, A_HEAD_DIM, A_HEAD_DIM)


def _rope_lanes(x, cos, sin_a, sin_b):
    n = x.shape[-1]
    half = QK_ROPE // 2
    return x * cos + pltpu.roll(x, n - half, 1) * sin_a + pltpu.roll(x, half, 1) * sin_b


def _mla_q_kernel(zq_ref, cos_ref, sa_ref, sb_ref, qn_ref, wuq_ref, gq_ref, kvn_ref, gk_ref,
                  q_ref, lat_ref, krg_ref):
    zq = zq_ref[0]
    qc = zq[:, 0:Q_LORA]
    ckv = zq[:, Q_LORA:Q_LORA + KV_LORA]
    kr = zq[:, Q_LORA + KV_LORA:ZQ_COLS]
    rms = lambda x, n: x * lax.rsqrt(jnp.sum(x * x, axis=-1, keepdims=True) * (1.0 / n) + NORM_EPS)
    qn = rms(qc, Q_LORA) * qn_ref[0]
    qf = _mm(qn, wuq_ref[0])
    cos, sin_a, sin_b = cos_ref[...], sa_ref[...], sb_ref[...]
    lane = lax.broadcasted_iota(jnp.int32, (1, HEAD_PAD), 1)
    is_nope = lane < QK_NOPE
    heads = []
    for h in range(B_HEADS):
        qh = qf[:, h * HEAD_PAD:(h + 1) * HEAD_PAD]
        sq = qh * qh
        ss_all = jnp.sum(sq, axis=-1, keepdims=True)
        ss_n = jnp.sum(jnp.where(is_nope, sq, 0.0), axis=-1, keepdims=True)
        scale = jnp.where(is_nope, lax.rsqrt(ss_n * (1.0 / QK_NOPE) + NORM_EPS),
                          lax.rsqrt((ss_all - ss_n) * (1.0 / QK_ROPE) + NORM_EPS))
        qh = qh * scale * gq_ref[0]
        heads.append(_rope_lanes(qh, cos, sin_a, sin_b))
    q_ref[0] = (jnp.concatenate(heads, axis=1) * (QK_DIM ** -0.5)).astype(BF16)
    lat_ref[0] = rms(ckv, KV_LORA) * kvn_ref[0]
    krn = rms(kr, QK_ROPE) * gk_ref[0]
    krg_ref[0] = _rope_lanes(krn, cos, sin_a, sin_b)


def _mla_q(zq, tabs, wts, layer, tm):
    bv, lv, _ = zq.shape
    row = lambda w: pl.BlockSpec((1, tm, w), lambda b, j: (b, j, 0))
    tab = pl.BlockSpec((tm, LANES), lambda b, j: (j, 0))
    par = lambda r, c: pl.BlockSpec((1, r, c), lambda b, j: (layer, 0, 0))
    return pl.pallas_call(
        _mla_q_kernel,
        grid=(bv, lv // tm),
        in_specs=[row(ZQ_COLS), tab, tab, tab, par(1, Q_LORA), par(Q_LORA, QK_WIDTH), par(1, HEAD_PAD),
                  par(1, KV_LORA), par(1, LANES)],
        out_specs=[row(QK_WIDTH), row(KV_LORA), row(LANES)],
        out_shape=[jax.ShapeDtypeStruct((bv, lv, QK_WIDTH), BF16),
                   jax.ShapeDtypeStruct((bv, lv, KV_LORA), F32),
                   jax.ShapeDtypeStruct((bv, lv, LANES), F32)],
        compiler_params=_cparams(("parallel", "parallel")),
        name="mla_q",
    )(zq, tabs[0], tabs[1], tabs[2], wts["q_norm"], wts["w_uq"], wts["g_q"], wts["kv_norm"], wts["g_kr"])


def _mla_expand_kernel(lat_ref, krg_ref, wk_ref, wv_ref, gk_ref, k_ref, v_ref):
    lat = lat_ref[0].astype(BF16)
    kf = jnp.dot(lat, wk_ref[0], preferred_element_type=F32)
    krg = krg_ref[0]
    heads = []
    for h in range(B_HEADS):
        kh = kf[:, h * HEAD_PAD:(h + 1) * HEAD_PAD]
        ss = jnp.sum(kh * kh, axis=-1, keepdims=True)
        heads.append(kh * lax.rsqrt(ss * (1.0 / QK_NOPE) + NORM_EPS) * gk_ref[0] + krg)
    k_ref[0] = jnp.concatenate(heads, axis=1).astype(BF16)
    v_ref[0] = jnp.dot(lat, wv_ref[0], preferred_element_type=F32).astype(BF16)


def _mla_expand(lat, krg, wts, layer, tm):
    bv, lv, _ = lat.shape
    row = lambda w: pl.BlockSpec((1, tm, w), lambda b, j: (b, j, 0))
    par = lambda r, c: pl.BlockSpec((1, r, c), lambda b, j: (layer, 0, 0))
    return pl.pallas_call(
        _mla_expand_kernel,
        grid=(bv, lv // tm),
        in_specs=[row(KV_LORA), row(LANES), par(KV_LORA, QK_WIDTH), par(KV_LORA, B_WIDTH), par(1, HEAD_PAD)],
        out_specs=[row(QK_WIDTH), row(B_WIDTH)],
        out_shape=[jax.ShapeDtypeStruct((bv, lv, QK_WIDTH), BF16),
                   jax.ShapeDtypeStruct((bv, lv, B_WIDTH), BF16)],
        compiler_params=_cparams(("parallel", "parallel")),
        name="mla_expand",
    )(lat, krg, wts["w_uk"], wts["w_uv"], wts["g_kn"])


def _attn_kernel(q_ref, k_ref, v_ref, gb_ref, o_ref, m_ref, l_ref, acc_ref, *,
                 tq, tk, causal, k_lo, k_hi):
    qi, ki = pl.program_id(1), pl.program_id(2)
    nk = pl.num_programs(2)

    @pl.when(ki == 0)
    def _():
        m_ref[...] = jnp.full(m_ref.shape, NEG_INF, F32)
        l_ref[...] = jnp.zeros(l_ref.shape, F32)
        acc_ref[...] = jnp.zeros(acc_ref.shape, F32)

    def step():
        kpos = ki * tk + lax.broadcasted_iota(jnp.int32, (tq, tk), 1)
        vis = (kpos >= k_lo) & (kpos < k_hi)
        if causal:
            qpos = qi * tq + lax.broadcasted_iota(jnp.int32, (tq, tk), 0)
            vis = vis & ((kpos // CHUNK) <= (qpos // CHUNK))
        lane = lax.broadcasted_iota(jnp.int32, (1, 2 * V_DIM), 1)
        low = lane < V_DIM
        for pair in range(B_HEADS // 2):
            vp = v_ref[0, :, pair * 2 * V_DIM:(pair + 1) * 2 * V_DIM]
            alphas, pvs = [], []
            for h in (2 * pair, 2 * pair + 1):
                s = _mm_nt(q_ref[0, :, h * HEAD_PAD:(h + 1) * HEAD_PAD],
                           k_ref[0, :, h * HEAD_PAD:(h + 1) * HEAD_PAD])
                s = jnp.where(vis, s, NEG_INF)
                m_old = m_ref[h]
                m_new = jnp.maximum(m_old, jnp.max(s, axis=-1, keepdims=True))
                alpha = jnp.exp(m_old - m_new)
                p = jnp.exp(s - m_new)
                l_ref[h] = alpha * l_ref[h] + jnp.sum(p, axis=-1, keepdims=True)
                m_ref[h] = m_new
                alphas.append(alpha)
                pvs.append(jnp.dot(p.astype(BF16), vp, preferred_element_type=F32))
            acc_ref[pair] = (jnp.where(low, alphas[0], alphas[1]) * acc_ref[pair]
                             + jnp.where(low, pvs[0], pvs[1]))

    if causal:
        pl.when(ki * tk < (qi + 1) * tq)(step)
    else:
        step()

    @pl.when(ki == nk - 1)
    def _():
        lane = lax.broadcasted_iota(jnp.int32, (1, 2 * V_DIM), 1)
        low = lane < V_DIM
        outs = []
        for pair in range(B_HEADS // 2):
            inv = jnp.where(low, 1.0 / l_ref[2 * pair], 1.0 / l_ref[2 * pair + 1])
            outs.append(acc_ref[pair] * inv)
        gb = gb_ref[0]
        o_ref[0] = (jnp.concatenate(outs, axis=1) * (gb * _sigmoid(gb))).astype(BF16)


def _attention(q, k, v, gb, tq, tk, causal, k_lo, k_hi):
    bs, lq, _ = q.shape
    lk = k.shape[1]
    if causal:
        kv_idx = lambda b, i, j: (b, jnp.minimum(j, ((i + 1) * tq - 1) // tk), 0)
    else:
        kv_idx = lambda b, i, j: (b, j, 0)
    return pl.pallas_call(
        functools.partial(_attn_kernel, tq=tq, tk=tk, causal=causal, k_lo=k_lo, k_hi=k_hi),
        grid=(bs, lq // tq, lk // tk),
        in_specs=[pl.BlockSpec((1, tq, QK_WIDTH), lambda b, i, j: (b, i, 0)),
                  pl.BlockSpec((1, tk, QK_WIDTH), kv_idx),
                  pl.BlockSpec((1, tk, B_WIDTH), kv_idx),
                  pl.BlockSpec((1, tq, B_WIDTH), lambda b, i, j: (b, i, 0))],
        out_specs=pl.BlockSpec((1, tq, B_WIDTH), lambda b, i, j: (b, i, 0)),
        out_shape=jax.ShapeDtypeStruct((bs, lq, B_WIDTH), BF16),
        scratch_shapes=[pltpu.VMEM((B_HEADS, tq, 1), F32), pltpu.VMEM((B_HEADS, tq, 1), F32),
                        pltpu.VMEM((B_HEADS // 2, tq, 2 * V_DIM), F32)],
        compiler_params=_cparams(("parallel", "parallel", "arbitrary")),
        name="attention",
    )(q, k, v, gb)


def _out_proj_kernel(x_ref, ya_ref, yb_ref, zg_ref, wa_ref, wb_ref, wo_ref, o_ref):
    zg = zg_ref[0]
    ga = _sigmoid(zg[:, 0:D_MODEL])
    gb = _sigmoid(zg[:, D_MODEL:G_COLS])
    t = (ga * jnp.dot(ya_ref[0], wa_ref[0], preferred_element_type=F32)
         + gb * jnp.dot(yb_ref[0], wb_ref[0], preferred_element_type=F32))
    o_ref[0] = x_ref[0] + _mm(t, wo_ref[0])


def _out_proj(x, ya, yb, zg, wts, layer, tm):
    bv, lv, _ = x.shape
    row = lambda w: pl.BlockSpec((1, tm, w), lambda b, j: (b, j, 0))
    par = lambda r, c: pl.BlockSpec((1, r, c), lambda b, j: (layer, 0, 0))
    return pl.pallas_call(
        _out_proj_kernel,
        grid=(bv, lv // tm),
        in_specs=[row(D_MODEL), row(A_WIDTH), row(B_WIDTH), row(G_COLS),
                  par(A_WIDTH, D_MODEL), par(B_WIDTH, D_MODEL), par(D_MODEL, D_MODEL)],
        out_specs=row(D_MODEL),
        out_shape=jax.ShapeDtypeStruct((bv, lv, D_MODEL), F32),
        compiler_params=_cparams(("parallel", "parallel")),
        name="out_proj",
    )(x, ya, yb, zg, wts["w_a"], wts["w_b"], wts["w_o"])


def _prep_weights(w_in, shift_mix, rwkv_w0, rwkv_w2, rwkv_a0, rwkv_a2, rwkv_k_k, rwkv_k_a, rwkv_r_k,
                  rwkv_ln_w, rwkv_ln_b, mla_q_norm, mla_w_uq, mla_kv_norm, mla_w_ukv, mla_qn_nope,
                  mla_kn_nope, mla_qn_rope, mla_kn_rope, w_branch_a, w_branch_b, w_out):
    depth = w_in.shape[0]
    a_end, b0 = A_COLS, A_COLS
    qc = w_in[:, :, b0:b0 + Q_LORA + KV_LORA]
    kr = w_in[:, :, b0 + Q_LORA + KV_LORA:b0 + Q_LORA + KV_LORA + QK_ROPE]
    gate_b = w_in[:, :, b0 + Q_LORA + KV_LORA + QK_ROPE:b0 + B_COLS]
    kr_grp = jnp.pad(kr, ((0, 0), (0, 0), (ROPE_LO, LANES - ROPE_LO - QK_ROPE)))
    w_in_p = jnp.concatenate([w_in[:, :, :a_end], qc, kr_grp, gate_b, w_in[:, :, A_COLS + B_COLS:]],
                             axis=-1).astype(BF16)
    zeros = jnp.zeros((depth, DECAY_LORA, A_WIDTH), F32)
    lora = jnp.concatenate([jnp.concatenate([rwkv_w2, zeros], axis=2),
                            jnp.concatenate([zeros, rwkv_a2], axis=2)], axis=1).astype(BF16)
    row = lambda p: p.reshape(depth, 1, -1)
    w_uq = mla_w_uq.reshape(depth, Q_LORA, B_HEADS, QK_DIM)
    w_uq = jnp.pad(w_uq, ((0, 0), (0, 0), (0, 0), (0, HEAD_PAD - QK_DIM))).reshape(depth, Q_LORA, QK_WIDTH)
    g_q = jnp.pad(jnp.concatenate([mla_qn_nope, mla_qn_rope], axis=1), ((0, 0), (0, HEAD_PAD - QK_DIM)))
    g_kr = jnp.pad(mla_kn_rope, ((0, 0), (ROPE_LO, LANES - ROPE_LO - QK_ROPE)))
    g_kn = jnp.pad(mla_kn_nope, ((0, 0), (0, HEAD_PAD - QK_NOPE)))
    w_ukv = mla_w_ukv.reshape(depth, KV_LORA, B_HEADS, QK_NOPE + V_DIM)
    w_uk = jnp.pad(w_ukv[..., :QK_NOPE], ((0, 0), (0, 0), (0, 0), (0, HEAD_PAD - QK_NOPE)))
    w_uv = w_ukv[..., QK_NOPE:]
    return dict(
        w_in=w_in_p, shift_mix=shift_mix, w0=row(rwkv_w0), a0=row(rwkv_a0), lora=lora,
        k_k=row(rwkv_k_k), k_a=row(rwkv_k_a), r_k=row(rwkv_r_k), ln_w=row(rwkv_ln_w), ln_b=row(rwkv_ln_b),
        q_norm=row(mla_q_norm), w_uq=w_uq.astype(BF16), g_q=row(g_q), kv_norm=row(mla_kv_norm),
        g_kr=row(g_kr), g_kn=row(g_kn),
        w_uk=w_uk.reshape(depth, KV_LORA, QK_WIDTH).astype(BF16),
        w_uv=w_uv.reshape(depth, KV_LORA, B_WIDTH).astype(BF16),
        w_a=w_branch_a.astype(BF16), w_b=w_branch_b.astype(BF16), w_o=w_out.astype(BF16))


def _rope_tables(t):
    half = QK_ROPE // 2
    inv = ROPE_THETA ** (-jnp.arange(0, QK_ROPE, 2, dtype=F32) / QK_ROPE)
    ang = t.astype(F32)[:, None] * inv[None, :]
    cos, sin = jnp.cos(ang), jnp.sin(ang)
    n = t.shape[0]
    ones = lambda w: jnp.ones((n, w), F32)
    zeros = lambda w: jnp.zeros((n, w), F32)
    tail = LANES - ROPE_LO - QK_ROPE
    cos_t = jnp.concatenate([ones(ROPE_LO), cos, cos, ones(tail)], axis=1)
    sin_a = jnp.concatenate([zeros(ROPE_LO), -sin, zeros(half), zeros(tail)], axis=1)
    sin_b = jnp.concatenate([zeros(ROPE_LO), zeros(half), sin, zeros(tail)], axis=1)
    return cos_t, sin_a, sin_b


def _largest_tile(n, cap, mult):
    best = mult
    for t in range(mult, min(n, cap) + 1, mult):
        if n % t == 0:
            best = t
    return best


def _layer(x, layer, wts, tabs, cfg, shift_row, s0, past=None):
    bv, lv, _ = x.shape
    bs, ls = cfg["bs"], cfg["ls"]
    seq = lambda a: a.reshape(bs, ls, a.shape[-1])
    za, zq, zgb, zg = _in_proj(x, wts["norm_w"], wts["w_in"], layer, cfg["tm"], cfg["n_pad"])
    za_s = seq(za)
    prep = _rwkv_prep(za_s, shift_row, wts, layer, cfg["tm_prep"], cfg["chunk"])
    ya, s_bd = _rwkv_chain(prep, _state_to_bd(s0, cfg["g"]), wts, layer, cfg["nb"], cfg["chunk"], cfg["g"])
    q, lat, krg = _mla_q(zq, tabs, wts, layer, cfg["tm"])
    if past is None:
        k, v = _mla_expand(lat, krg, wts, layer, cfg["tm"])
        k, v = seq(k), seq(v)
        k_hi = ls
    else:
        past_lat, past_kr = past
        n_past = past_lat.shape[1]
        k_hi = n_past + ls
        lk = -(-k_hi // LANES) * LANES
        kr_grp = jnp.pad(past_kr, ((0, 0), (0, 0), (ROPE_LO, LANES - ROPE_LO - QK_ROPE)))
        pad = ((0, 0), (0, lk - k_hi), (0, 0))
        lat_all = jnp.pad(jnp.concatenate([past_lat, seq(lat)], axis=1), pad)
        krg_all = jnp.pad(jnp.concatenate([kr_grp, seq(krg)], axis=1), pad)
        k, v = _mla_expand(lat_all, krg_all, wts, layer, _largest_tile(lk, 1024, LANES))
    yb = _attention(seq(q), k, v, seq(zgb), cfg["tq"], cfg["tk"] if past is None else k.shape[1],
                    past is None, cfg["k_lo"], k_hi)
    x_new = _out_proj(x, ya.reshape(bv, lv, A_WIDTH), yb.reshape(bv, lv, B_WIDTH), zg, wts, layer, cfg["tm"])
    return x_new, _state_from_bd(s_bd, cfg["g"]), za_s[:, -1:, :], seq(lat), seq(krg)[:, :, ROPE_LO:ROPE_LO + QK_ROPE]


def kernel(x_prompt, x_sample, state_rwkv, state_shift, cache_mla_latent, cache_mla_krope, meta_tokens, norm_w, w_in, shift_mix, rwkv_w0, rwkv_w2, rwkv_a0, rwkv_a2, rwkv_k_k, rwkv_k_a, rwkv_r_k, rwkv_ln_w, rwkv_ln_b, mla_q_norm, mla_w_uq, mla_kv_norm, mla_w_ukv, mla_qn_nope, mla_kn_nope, mla_qn_rope, mla_kn_rope, w_branch_a, w_branch_b, w_out):
    depth = w_in.shape[0]
    bp, seq_len, _ = x_prompt.shape
    bd, dec_len, _ = x_sample.shape
    past_len = cache_mla_latent.shape[2]
    assert seq_len % LANES == 0 and dec_len % 8 == 0
    wts = _prep_weights(w_in, shift_mix, rwkv_w0, rwkv_w2, rwkv_a0, rwkv_a2, rwkv_k_k, rwkv_k_a, rwkv_r_k,
                        rwkv_ln_w, rwkv_ln_b, mla_q_norm, mla_w_uq, mla_kv_norm, mla_w_ukv, mla_qn_nope,
                        mla_kn_nope, mla_qn_rope, mla_kn_rope, w_branch_a, w_branch_b, w_out)
    wts["norm_w"] = norm_w.reshape(depth, 1, D_MODEL)

    n_zero = PAD_FRONT - N_META
    lp = PAD_FRONT + seq_len
    meta = jnp.broadcast_to(meta_tokens[None].astype(F32), (bp, N_META, D_MODEL))
    xp = jnp.concatenate([jnp.zeros((bp, n_zero, D_MODEL), F32), meta, x_prompt], axis=1)
    tabs_p = _rope_tables(jnp.arange(lp, dtype=jnp.int32) - PAD_FRONT)
    tile_p = _largest_tile(lp, 512, LANES)
    cfg_p = dict(bs=bp, ls=lp, tm=tile_p, n_pad=n_zero, tm_prep=_largest_tile(lp, 256, CHUNK),
                 chunk=CHUNK, g=4, nb=1, tq=tile_p, tk=tile_p, k_lo=n_zero)
    zero_row = jnp.zeros((bp, 1, A_COLS), F32)
    zero_state = jnp.zeros((bp, A_HEADS, A_HEAD_DIM, A_HEAD_DIM), F32)

    n_rows = bd * dec_len
    xs = x_sample.reshape(1, n_rows, D_MODEL)
    t_s = past_len + (jnp.arange(n_rows, dtype=jnp.int32) % dec_len)
    tabs_s = _rope_tables(t_s)
    cfg_s = dict(bs=bd, ls=dec_len, tm=_largest_tile(n_rows, 512, 8), n_pad=0, tm_prep=dec_len,
                 chunk=dec_len, g=A_HEADS, nb=1, tq=dec_len, tk=None, k_lo=0)

    outs_p, outs_s = [], []
    for l in range(depth):
        xp, *op = _layer(xp, l, wts, tabs_p, cfg_p, zero_row, zero_state)
        xs, *os_ = _layer(xs, l, wts, tabs_s, cfg_s, state_shift[l], state_rwkv[l],
                          past=(cache_mla_latent[l], cache_mla_krope[l]))
        outs_p.append(op)
        outs_s.append(os_)
    stack = lambda outs, i: jnp.stack([o[i] for o in outs])
    first = PAD_FRONT - N_META
    return (xp[:, PAD_FRONT:], xs.reshape(bd, dec_len, D_MODEL),
            stack(outs_p, 0), stack(outs_p, 1), stack(outs_p, 2)[:, :, first:], stack(outs_p, 3)[:, :, first:],
            stack(outs_s, 0), stack(outs_s, 1), stack(outs_s, 2), stack(outs_s, 3))
```

```python
import functools

import numpy as np
import jax
import jax.numpy as jnp
from jax import lax
from jax.experimental import pallas as pl
from jax.experimental.pallas import tpu as pltpu

F32 = jnp.float32
BF16 = jnp.bfloat16

D_MODEL = 1024
CHUNK = 64
N_META = 16
A_HEADS = 8
A_HEAD_DIM = 64
A_WIDTH = A_HEADS * A_HEAD_DIM
DECAY_LORA = 64
ICLR_LORA = 64
B_HEADS = 8
QK_NOPE = 64
QK_ROPE = 32
QK_DIM = QK_NOPE + QK_ROPE
V_DIM = 64
B_WIDTH = B_HEADS * V_DIM
Q_LORA = 384
KV_LORA = 256
ROPE_THETA = 10000.0
NORM_EPS = 1e-6
GN_EPS = 64e-5
NEG_INF = -1e30
A_COLS = 4 * A_WIDTH + DECAY_LORA + ICLR_LORA
B_COLS = Q_LORA + KV_LORA + QK_ROPE + B_WIDTH
G_COLS = 2 * D_MODEL

LANES = 128
HEAD_PAD = LANES
QK_WIDTH = B_HEADS * HEAD_PAD
ZQ_COLS = Q_LORA + KV_LORA + LANES
ROPE_LO = QK_NOPE
PAD_FRONT = 128
VMEM_LIMIT = 56 * 1024 * 1024
BASE_BLOCK = 8


def _cparams(sem):
    return pltpu.CompilerParams(dimension_semantics=sem, vmem_limit_bytes=VMEM_LIMIT)


def _mm(a, b):
    return jnp.dot(a.astype(BF16), b.astype(BF16), preferred_element_type=F32)


def _mm_nt(a, b):
    return lax.dot_general(a.astype(BF16), b.astype(BF16), (((1,), (1,)), ((), ())),
                           preferred_element_type=F32)


def _mm_tn(a, b):
    return lax.dot_general(a.astype(BF16), b.astype(BF16), (((0,), (0,)), ((), ())),
                           preferred_element_type=F32)


def _split3(x):
    hi = x.astype(BF16)
    r1 = x - hi.astype(F32)
    mid = r1.astype(BF16)
    lo = (r1 - mid.astype(F32)).astype(BF16)
    return hi, mid, lo


def _mm_x_exact(x, e):
    hi, mid, lo = _split3(x)
    d = lambda u: jnp.dot(u, e, preferred_element_type=F32)
    return d(hi) + d(mid) + d(lo)


def _mm_exact_x(e, x):
    hi, mid, lo = _split3(x)
    d = lambda u: jnp.dot(e, u, preferred_element_type=F32)
    return d(hi) + d(mid) + d(lo)


def _sigmoid(x):
    return 1.0 / (1.0 + jnp.exp(-x))


def _in_proj_kernel(x_ref, nw_ref, w_ref, za_ref, zq_ref, zgb_ref, zg_ref, *, tm, n_pad):
    x = x_ref[0]
    ms = jnp.mean(x * x, axis=-1, keepdims=True)
    h = x * lax.rsqrt(ms + NORM_EPS) * nw_ref[0]
    if n_pad:
        row = pl.program_id(1) * tm + lax.broadcasted_iota(jnp.int32, (tm, 1), 0)
        h = jnp.where(row >= n_pad, h, 0.0)
    hb = h.astype(BF16)
    o = 0
    for ref, width in ((za_ref, A_COLS), (zq_ref, ZQ_COLS), (zgb_ref, B_WIDTH), (zg_ref, G_COLS)):
        ref[0] = jnp.dot(hb, w_ref[0, :, o:o + width], preferred_element_type=F32)
        o += width


def _in_proj(x, norm_w, w_in_p, layer, tm, n_pad):
    bv, lv, _ = x.shape
    cols = w_in_p.shape[-1]
    row = lambda w: pl.BlockSpec((1, tm, w), lambda b, j: (b, j, 0))
    return pl.pallas_call(
        functools.partial(_in_proj_kernel, tm=tm, n_pad=n_pad),
        grid=(bv, lv // tm),
        in_specs=[row(D_MODEL),
                  pl.BlockSpec((1, 1, D_MODEL), lambda b, j: (layer, 0, 0)),
                  pl.BlockSpec((1, D_MODEL, cols), lambda b, j: (layer, 0, 0))],
        out_specs=[row(A_COLS), row(ZQ_COLS), row(B_WIDTH), row(G_COLS)],
        out_shape=[jax.ShapeDtypeStruct((bv, lv, w), F32) for w in (A_COLS, ZQ_COLS, B_WIDTH, G_COLS)],
        compiler_params=_cparams(("parallel", "parallel")),
        name="in_proj",
    )(x, norm_w, w_in_p)


def _rwkv_prep_kernel(za_ref, zap_ref, sh_ref, mix_ref, w0_ref, a0_ref, lora_ref, kk_ref, ka_ref,
                      rk_ref, e_ref, tri_ref, same_ref,
                      at_ref, rt_ref, bt_ref, kt_ref, vb_ref, pc_ref, bonus_ref, sg_ref, *, tm):
    j = pl.program_id(1)
    a = za_ref[0]
    first = jnp.where(j == 0, sh_ref[0], zap_ref[0, 7:8, :])
    rowid = lax.broadcasted_iota(jnp.int32, (tm, 1), 0)
    prev = jnp.where(rowid == 0, first, pltpu.roll(a, 1, 0))
    xs = a * mix_ref[0, 0:1, :] + prev * mix_ref[0, 1:2, :]
    r = xs[:, 0:A_WIDTH]
    k = xs[:, A_WIDTH:2 * A_WIDTH]
    v = xs[:, 2 * A_WIDTH:3 * A_WIDTH]
    gate = xs[:, 3 * A_WIDTH:4 * A_WIDTH]
    lr = xs[:, 4 * A_WIDTH:A_COLS]
    lane = lax.broadcasted_iota(jnp.int32, (1, DECAY_LORA + ICLR_LORA), 1)
    lr = jnp.where(lane < DECAY_LORA, jnp.tanh(lr), lr)
    lo = _mm(lr, lora_ref[0])
    wlin = w0_ref[0] + lo[:, 0:A_WIDTH]
    alin = a0_ref[0] + lo[:, A_WIDTH:2 * A_WIDTH]
    softplus_neg = jnp.maximum(-wlin, 0.0) + jnp.log1p(jnp.exp(-jnp.abs(wlin)))
    logdec = -jnp.exp(-softplus_neg - 0.5)
    av = _sigmoid(alin)
    e = e_ref[...]
    kk = k * kk_ref[0]
    kk = kk * lax.rsqrt(jnp.maximum(_mm_x_exact(kk * kk, e), 1e-24))
    kmod = k * (1.0 + (av - 1.0) * ka_ref[0])
    cum = _mm_exact_x(tri_ref[...], logdec)
    tot = _mm_exact_x(same_ref[...], logdec)
    pin = jnp.exp(cum)
    pex = jnp.exp(cum - logdec)
    pinv = jnp.exp(-cum)
    at_ref[0] = (-kk * pex).astype(BF16)
    rt_ref[0] = (r * pin).astype(BF16)
    bt_ref[0] = (kk * av * pinv).astype(BF16)
    kt_ref[0] = (kmod * pinv).astype(BF16)
    vb_ref[0] = v.astype(BF16)
    pc_ref[0] = jnp.exp(tot)
    bonus_ref[0] = _mm_x_exact(r * kmod * rk_ref[0], e) * v
    sg_ref[0] = gate * _sigmoid(gate)


def _rwkv_prep(za, shift_row, wts, layer, tm, chunk):
    bs, ls, _ = za.shape
    ids = np.arange(tm)
    same = (ids[:, None] // chunk) == (ids[None, :] // chunk)
    tri = same & (ids[None, :] <= ids[:, None])
    e = (np.arange(A_WIDTH)[:, None] // A_HEAD_DIM) == (np.arange(A_WIDTH)[None, :] // A_HEAD_DIM)
    row = lambda w: pl.BlockSpec((1, tm, w), lambda b, j: (b, j, 0))
    par = lambda r, c: pl.BlockSpec((1, r, c), lambda b, j: (layer, 0, 0))
    const = lambda r, c: pl.BlockSpec((r, c), lambda b, j: (0, 0))
    nsub = tm // 8
    outs = [jax.ShapeDtypeStruct((bs, ls, A_WIDTH), BF16)] * 5 + [jax.ShapeDtypeStruct((bs, ls, A_WIDTH), F32)] * 3
    return pl.pallas_call(
        functools.partial(_rwkv_prep_kernel, tm=tm),
        grid=(bs, ls // tm),
        in_specs=[row(A_COLS),
                  pl.BlockSpec((1, 8, A_COLS), lambda b, j: (b, jnp.maximum(j * nsub - 1, 0), 0)),
                  pl.BlockSpec((1, 1, A_COLS), lambda b, j: (b, 0, 0)),
                  par(2, A_COLS), par(1, A_WIDTH), par(1, A_WIDTH),
                  par(DECAY_LORA + ICLR_LORA, 2 * A_WIDTH),
                  par(1, A_WIDTH), par(1, A_WIDTH), par(1, A_WIDTH),
                  const(A_WIDTH, A_WIDTH), const(tm, tm), const(tm, tm)],
        out_specs=[row(A_WIDTH)] * 8,
        out_shape=outs,
        compiler_params=_cparams(("parallel", "parallel")),
        name="rwkv_prep",
    )(za, za, shift_row, wts["shift_mix"], wts["w0"], wts["a0"], wts["lora"], wts["k_k"], wts["k_a"],
      wts["r_k"], jnp.asarray(e, BF16), jnp.asarray(tri, BF16), jnp.asarray(same, BF16))


def _chain_masks(chunk, g):
    gc, w = g * chunk, g * A_HEAD_DIM
    t = np.arange(chunk)[:, None]
    s = np.arange(gc)[None, :] % chunk
    wide = {"strict": s < t, "incl": s <= t, "eye": s == t,
            "base": (s // BASE_BLOCK == t // BASE_BLOCK) & (s < t)}
    levels = []
    sz = BASE_BLOCK
    while sz < chunk:
        levels.append((s // (2 * sz) == t // (2 * sz)) & (s // sz != t // sz) & (s < t))
        sz *= 2
    rows = np.arange(gc)[:, None] // chunk
    bd_nat = rows == (np.arange(w)[None, :] // A_HEAD_DIM)
    bd_wide = rows == (np.arange(gc)[None, :] // chunk)
    bd_state = (np.arange(w)[:, None] // A_HEAD_DIM) == (np.arange(w)[None, :] // A_HEAD_DIM)
    f = lambda m: jnp.asarray(m, F32)
    b = lambda m: jnp.asarray(m, BF16)
    lev = np.stack(levels) if levels else np.zeros((1, chunk, gc), bool)
    return dict(strict=f(wide["strict"]), incl=f(wide["incl"]), eye=f(wide["eye"]), base=f(wide["base"]),
                lev=f(lev), bd_nat=b(bd_nat), bd_wide=b(bd_wide), bd_state=f(bd_state)), len(levels)


def _rwkv_chain_kernel(at_ref, rt_ref, bt_ref, kt_ref, vb_ref, pc_ref, bonus_ref, sg_ref, s0_ref,
                       lnw_ref, lnb_ref, e_ref, strict_ref, incl_ref, eye_ref, base_ref, lev_ref,
                       bdn_ref, bdw_ref, bds_ref, ya_ref, s_ref, *, nb, nc, chunk, g, n_lev):
    c = chunk
    gc, w = g * c, g * A_HEAD_DIM
    n_groups = A_HEADS // g

    @pl.when(pl.program_id(1) == 0)
    def _():
        s_ref[...] = s0_ref[...]

    strict, incl, eye, base = strict_ref[...], incl_ref[...], eye_ref[...], base_ref[...]
    bdn, bdw, bds = bdn_ref[...], bdw_ref[...], bds_ref[...]

    def bd(x, mask):
        return jnp.concatenate([x.astype(BF16)] * g, axis=0) * mask

    def wide_mm(x, y):
        return jnp.dot(x.astype(BF16), bd(y, bdw), preferred_element_type=F32)

    chains = [(b, grp) for b in range(nb) for grp in range(n_groups)]
    lanes = [slice(grp * w, (grp + 1) * w) for _, grp in chains]
    each = lambda fn, *lists: [fn(*args) for args in zip(*lists)]
    dot = lambda x, y: jnp.dot(x.astype(BF16), y, preferred_element_type=F32)
    cat0 = lambda x, y: jnp.concatenate([x, y], axis=0)
    states = [s_ref[b, grp] for b, grp in chains]
    y_chunks = [[] for _ in chains]
    for ci in range(nc):
        rows = slice(ci * c, (ci + 1) * c)
        load = lambda ref: [ref[b, rows, sl] for (b, _), sl in zip(chains, lanes)]
        at, rt, bt, kt, vb = (load(ref) for ref in (at_ref, rt_ref, bt_ref, kt_ref, vb_ref))
        ar = each(cat0, at, rt)
        gm = each(lambda a, b_, k_: _mm_nt(a, cat0(bd(b_, bdn), bd(k_, bdn))), ar, bt, kt)
        m_ab = [g_[:c, :gc] * strict for g_ in gm]
        d1 = [m * base for m in m_ab]
        d2 = each(wide_mm, d1, d1)
        t = [eye + d for d in d1]
        t = each(lambda t_, d: t_ + wide_mm(t_, d), t, d2)
        d4 = each(wide_mm, d2, d2)
        t = each(lambda t_, d: t_ + wide_mm(t_, d), t, d4)
        for lv in range(n_lev):
            off = each(lambda m, t_: wide_mm(m * lev_ref[lv], t_), m_ab, t)
            t = each(lambda t_, o: t_ + wide_mm(t_, o), t, off)
        vbd = [bd(v, bdn) for v in vb]
        akv = each(lambda g_, v: dot(g_[:c, gc:] * strict, v), gm, vbd)
        ah = each(_mm_nt, ar, states)
        u = each(lambda t_, a, k_: dot(t_, bd(a[:c] + k_, bdn)).astype(BF16), t, ah, akv)
        y = each(lambda a, g_, u_, v: a[c:] + dot(g_[c:] * jnp.concatenate([incl, incl], axis=1),
                                                  cat0(bd(u_, bdn), v)), ah, gm, u, vbd)
        upd = each(lambda u_, v, b_, k_: _mm_tn(cat0(u_, v), cat0(b_, k_)), u, vb, bt, kt)
        states = [(s + up * bds) * pc_ref[b, ci * c:ci * c + 1, sl]
                  for s, up, (b, _), sl in zip(states, upd, chains, lanes)]
        for yc, y_ in zip(y_chunks, y):
            yc.append(y_)
    for (b, grp), s in zip(chains, states):
        s_ref[b, grp] = s
    seqs = [yc[0] if nc == 1 else jnp.concatenate(yc, axis=0) for yc in y_chunks]
    ys = [seqs[b * n_groups] if n_groups == 1 else jnp.concatenate(seqs[b * n_groups:(b + 1) * n_groups], axis=1)
          for b in range(nb)]
    y = ys[0] if nb == 1 else jnp.concatenate(ys, axis=0)
    e = e_ref[...]
    inv_n = 1.0 / A_HEAD_DIM
    mu = _mm_x_exact(y, e) * inv_n
    d = y - mu
    var = _mm_x_exact(d * d, e) * inv_n
    yn = d * lax.rsqrt(var + GN_EPS) * lnw_ref[0] + lnb_ref[0]
    bonus = bonus_ref[...].reshape(nb * nc * c, A_WIDTH)
    sg = sg_ref[...].reshape(nb * nc * c, A_WIDTH)
    ya_ref[...] = ((yn + bonus) * sg).astype(BF16).reshape(nb, nc * c, A_WIDTH)


def _rwkv_chain(prep, s0_bd, wts, layer, nb, nc, chunk, g):
    at, rt, bt, kt, vb, pc, bonus, sg = prep
    bs, ls, _ = at.shape
    gc, w = g * chunk, g * A_HEAD_DIM
    n_groups = A_HEADS // g
    masks, n_lev = _chain_masks(chunk, g)
    e = (np.arange(A_WIDTH)[:, None] // A_HEAD_DIM) == (np.arange(A_WIDTH)[None, :] // A_HEAD_DIM)
    row = pl.BlockSpec((nb, nc * chunk, A_WIDTH), lambda b, j: (b, j, 0))
    par = pl.BlockSpec((1, 1, A_WIDTH), lambda b, j: (layer, 0, 0))
    state = pl.BlockSpec((nb, n_groups, w, w), lambda b, j: (b, 0, 0, 0))
    const = lambda shape: pl.BlockSpec(shape, lambda b, j: (0,) * len(shape))
    return pl.pallas_call(
        functools.partial(_rwkv_chain_kernel, nb=nb, nc=nc, chunk=chunk, g=g, n_lev=n_lev),
        grid=(bs // nb, ls // (nc * chunk)),
        in_specs=[row] * 8 + [state, par, par, const((A_WIDTH, A_WIDTH)),
                              const((chunk, gc)), const((chunk, gc)), const((chunk, gc)), const((chunk, gc)),
                              const(tuple(masks["lev"].shape)),
                              const((gc, w)), const((gc, gc)), const((w, w))],
        out_specs=[row, state],
        out_shape=[jax.ShapeDtypeStruct((bs, ls, A_WIDTH), BF16),
                   jax.ShapeDtypeStruct((bs, n_groups, w, w), F32)],
        compiler_params=_cparams(("parallel", "arbitrary")),
        name="rwkv_chain",
    )(at, rt, bt, kt, vb, pc, bonus, sg, s0_bd, wts["ln_w"], wts["ln_b"], jnp.asarray(e, BF16),
      masks["strict"], masks["incl"], masks["eye"], masks["base"], masks["lev"],
      masks["bd_nat"], masks["bd_wide"], masks["bd_state"])


def _state_to_bd(s, g):
    b = s.shape[0]
    n_groups = A_HEADS // g
    s = s.reshape(b, n_groups, g, A_HEAD_DIM, A_HEAD_DIM)
    eye = jnp.eye(g, dtype=s.dtype)
    out = jnp.einsum("bnhvk,hi->bnhvik", s, eye)
    return out.reshape(b, n_groups, g * A_HEAD_DIM, g * A_HEAD_DIM)


def _state_from_bd(s_bd, g):
    b, n_groups = s_bd.shape[:2]
    s = s_bd.reshape(b, n_groups, g, A_HEAD_DIM, g, A_HEAD_DIM)
    s = jnp.stack([s[:, :, h, :, h, :] for h in range(g)], axis=2)
    return s.reshape(b, A_HEADS, A_HEAD_DIM, A_HEAD_DIM)


def _rope_lanes(x, cos, sin_a, sin_b):
    n = x.shape[-1]
    half = QK_ROPE // 2
    return x * cos + pltpu.roll(x, n - half, 1) * sin_a + pltpu.roll(x, half, 1) * sin_b


def _mla_q_kernel(zq_ref, cos_ref, sa_ref, sb_ref, qn_ref, wuq_ref, gq_ref, kvn_ref, gk_ref,
                  q_ref, lat_ref, krg_ref):
    zq = zq_ref[0]
    qc = zq[:, 0:Q_LORA]
    ckv = zq[:, Q_LORA:Q_LORA + KV_LORA]
    kr = zq[:, Q_LORA + KV_LORA:ZQ_COLS]
    rms = lambda x, n: x * lax.rsqrt(jnp.sum(x * x, axis=-1, keepdims=True) * (1.0 / n) + NORM_EPS)
    qn = rms(qc, Q_LORA) * qn_ref[0]
    qf = _mm(qn, wuq_ref[0])
    cos, sin_a, sin_b = cos_ref[...], sa_ref[...], sb_ref[...]
    lane = lax.broadcasted_iota(jnp.int32, (1, HEAD_PAD), 1)
    is_nope = lane < QK_NOPE
    heads = []
    for h in range(B_HEADS):
        qh = qf[:, h * HEAD_PAD:(h + 1) * HEAD_PAD]
        sq = qh * qh
        ss_all = jnp.sum(sq, axis=-1, keepdims=True)
        ss_n = jnp.sum(jnp.where(is_nope, sq, 0.0), axis=-1, keepdims=True)
        scale = jnp.where(is_nope, lax.rsqrt(ss_n * (1.0 / QK_NOPE) + NORM_EPS),
                          lax.rsqrt((ss_all - ss_n) * (1.0 / QK_ROPE) + NORM_EPS))
        qh = qh * scale * gq_ref[0]
        heads.append(_rope_lanes(qh, cos, sin_a, sin_b))
    q_ref[0] = (jnp.concatenate(heads, axis=1) * (QK_DIM ** -0.5)).astype(BF16)
    lat_ref[0] = rms(ckv, KV_LORA) * kvn_ref[0]
    krn = rms(kr, QK_ROPE) * gk_ref[0]
    krg_ref[0] = _rope_lanes(krn, cos, sin_a, sin_b)


def _mla_q(zq, tabs, wts, layer, tm):
    bv, lv, _ = zq.shape
    row = lambda w: pl.BlockSpec((1, tm, w), lambda b, j: (b, j, 0))
    tab = pl.BlockSpec((tm, LANES), lambda b, j: (j, 0))
    par = lambda r, c: pl.BlockSpec((1, r, c), lambda b, j: (layer, 0, 0))
    return pl.pallas_call(
        _mla_q_kernel,
        grid=(bv, lv // tm),
        in_specs=[row(ZQ_COLS), tab, tab, tab, par(1, Q_LORA), par(Q_LORA, QK_WIDTH), par(1, HEAD_PAD),
                  par(1, KV_LORA), par(1, LANES)],
        out_specs=[row(QK_WIDTH), row(KV_LORA), row(LANES)],
        out_shape=[jax.ShapeDtypeStruct((bv, lv, QK_WIDTH), BF16),
                   jax.ShapeDtypeStruct((bv, lv, KV_LORA), F32),
                   jax.ShapeDtypeStruct((bv, lv, LANES), F32)],
        compiler_params=_cparams(("parallel", "parallel")),
        name="mla_q",
    )(zq, tabs[0], tabs[1], tabs[2], wts["q_norm"], wts["w_uq"], wts["g_q"], wts["kv_norm"], wts["g_kr"])


def _mla_expand_kernel(lat_ref, krg_ref, wk_ref, wv_ref, gk_ref, k_ref, v_ref):
    lat = lat_ref[0].astype(BF16)
    kf = jnp.dot(lat, wk_ref[0], preferred_element_type=F32)
    krg = krg_ref[0]
    heads = []
    for h in range(B_HEADS):
        kh = kf[:, h * HEAD_PAD:(h + 1) * HEAD_PAD]
        ss = jnp.sum(kh * kh, axis=-1, keepdims=True)
        heads.append(kh * lax.rsqrt(ss * (1.0 / QK_NOPE) + NORM_EPS) * gk_ref[0] + krg)
    k_ref[0] = jnp.concatenate(heads, axis=1).astype(BF16)
    lane = lax.broadcasted_iota(jnp.int32, (1, QK_WIDTH), 1)
    ones_col = jnp.where(lane % HEAD_PAD == V_DIM, 1.0, 0.0)
    v_ref[0] = (jnp.dot(lat, wv_ref[0], preferred_element_type=F32) + ones_col).astype(BF16)


def _mla_expand(lat, krg, wts, layer, tm):
    bv, lv, _ = lat.shape
    row = lambda w: pl.BlockSpec((1, tm, w), lambda b, j: (b, j, 0))
    par = lambda r, c: pl.BlockSpec((1, r, c), lambda b, j: (layer, 0, 0))
    return pl.pallas_call(
        _mla_expand_kernel,
        grid=(bv, lv // tm),
        in_specs=[row(KV_LORA), row(LANES), par(KV_LORA, QK_WIDTH), par(KV_LORA, QK_WIDTH), par(1, HEAD_PAD)],
        out_specs=[row(QK_WIDTH), row(QK_WIDTH)],
        out_shape=[jax.ShapeDtypeStruct((bv, lv, QK_WIDTH), BF16),
                   jax.ShapeDtypeStruct((bv, lv, QK_WIDTH), BF16)],
        compiler_params=_cparams(("parallel", "parallel")),
        name="mla_expand",
    )(lat, krg, wts["w_uk"], wts["w_uv"], wts["g_kn"])


def _attn_kernel(q_ref, k_ref, v_ref, gb_ref, bias_ref, o_ref, m_ref, acc_ref, *, tq, tk, causal):
    qi = pl.program_id(1)
    n_kv = qi + 1 if causal else k_ref.shape[1] // tk
    m_ref[...] = jnp.full(m_ref.shape, NEG_INF, F32)
    acc_ref[...] = jnp.zeros(acc_ref.shape, F32)

    def kv_step(c, carry):
        start = pl.multiple_of(c * tk, tk)
        if causal:
            bias = bias_ref[(c == qi).astype(jnp.int32) + 2 * (c == 0).astype(jnp.int32)]
        else:
            bias = bias_ref[0]
        for h in range(B_HEADS):
            lanes = slice(h * HEAD_PAD, (h + 1) * HEAD_PAD)
            s = _mm_nt(q_ref[0, :, lanes], k_ref[0, pl.ds(start, tk), lanes]) + bias
            m_old = m_ref[h]
            m_new = jnp.maximum(m_old, jnp.max(s, axis=-1, keepdims=True))
            p = jnp.exp(s - jnp.concatenate([m_new] * (tk // LANES), axis=1))
            pv = jnp.dot(p.astype(BF16), v_ref[0, pl.ds(start, tk), lanes], preferred_element_type=F32)
            acc_ref[h] = jnp.exp(m_old - m_new) * acc_ref[h] + pv
            m_ref[h] = m_new
        return carry

    lax.fori_loop(0, n_kv, kv_step, 0)
    outs = []
    for h in range(B_HEADS):
        acc = acc_ref[h]
        outs.append(acc[:, 0:V_DIM] / acc[:, V_DIM:V_DIM + 1])
    gb = gb_ref[0]
    o_ref[0] = (jnp.concatenate(outs, axis=1) * (gb * _sigmoid(gb))).astype(BF16)


def _attention(q, k, v, gb, bias, tq, tk, causal):
    bs, lq, _ = q.shape
    lk = k.shape[1]
    whole = lambda: pl.BlockSpec((1, lk, QK_WIDTH), lambda b, i: (b, 0, 0), pipeline_mode=pl.Buffered(1))
    return pl.pallas_call(
        functools.partial(_attn_kernel, tq=tq, tk=tk, causal=causal),
        grid=(bs, lq // tq),
        in_specs=[pl.BlockSpec((1, tq, QK_WIDTH), lambda b, i: (b, i, 0)),
                  whole(), whole(),
                  pl.BlockSpec((1, tq, B_WIDTH), lambda b, i: (b, i, 0)),
                  pl.BlockSpec(bias.shape, lambda b, i: (0, 0, 0))],
        out_specs=pl.BlockSpec((1, tq, B_WIDTH), lambda b, i: (b, i, 0)),
        out_shape=jax.ShapeDtypeStruct((bs, lq, B_WIDTH), BF16),
        scratch_shapes=[pltpu.VMEM((B_HEADS, tq, LANES), F32), pltpu.VMEM((B_HEADS, tq, HEAD_PAD), F32)],
        compiler_params=_cparams(("parallel", "arbitrary")),
        name="attention",
    )(q, k, v, gb, bias)


def _prompt_bias(t, n_invisible):
    i = np.arange(t)
    diag = np.where((i[None, :] // CHUNK) <= (i[:, None] // CHUNK), 0.0, NEG_INF)
    first = np.broadcast_to(np.where(i[None, :] >= n_invisible, 0.0, NEG_INF), (t, t))
    return jnp.asarray(np.stack([np.zeros((t, t)), diag, first, np.minimum(diag, first)]), F32)


def _tail_bias(tq, lk, k_hi):
    vis = np.arange(lk)[None, None, :] < k_hi
    return jnp.asarray(np.broadcast_to(np.where(vis, 0.0, NEG_INF), (1, tq, lk)), F32)


def _out_proj_kernel(x_ref, ya_ref, yb_ref, zg_ref, wa_ref, wb_ref, wo_ref, o_ref):
    zg = zg_ref[0]
    ga = _sigmoid(zg[:, 0:D_MODEL])
    gb = _sigmoid(zg[:, D_MODEL:G_COLS])
    t = (ga * jnp.dot(ya_ref[0], wa_ref[0], preferred_element_type=F32)
         + gb * jnp.dot(yb_ref[0], wb_ref[0], preferred_element_type=F32))
    o_ref[0] = x_ref[0] + _mm(t, wo_ref[0])


def _out_proj(x, ya, yb, zg, wts, layer, tm):
    bv, lv, _ = x.shape
    row = lambda w: pl.BlockSpec((1, tm, w), lambda b, j: (b, j, 0))
    par = lambda r, c: pl.BlockSpec((1, r, c), lambda b, j: (layer, 0, 0))
    return pl.pallas_call(
        _out_proj_kernel,
        grid=(bv, lv // tm),
        in_specs=[row(D_MODEL), row(A_WIDTH), row(B_WIDTH), row(G_COLS),
                  par(A_WIDTH, D_MODEL), par(B_WIDTH, D_MODEL), par(D_MODEL, D_MODEL)],
        out_specs=row(D_MODEL),
        out_shape=jax.ShapeDtypeStruct((bv, lv, D_MODEL), F32),
        compiler_params=_cparams(("parallel", "parallel")),
        name="out_proj",
    )(x, ya, yb, zg, wts["w_a"], wts["w_b"], wts["w_o"])


def _prep_weights(w_in, shift_mix, rwkv_w0, rwkv_w2, rwkv_a0, rwkv_a2, rwkv_k_k, rwkv_k_a, rwkv_r_k,
                  rwkv_ln_w, rwkv_ln_b, mla_q_norm, mla_w_uq, mla_kv_norm, mla_w_ukv, mla_qn_nope,
                  mla_kn_nope, mla_qn_rope, mla_kn_rope, w_branch_a, w_branch_b, w_out):
    depth = w_in.shape[0]
    a_end, b0 = A_COLS, A_COLS
    qc = w_in[:, :, b0:b0 + Q_LORA + KV_LORA]
    kr = w_in[:, :, b0 + Q_LORA + KV_LORA:b0 + Q_LORA + KV_LORA + QK_ROPE]
    gate_b = w_in[:, :, b0 + Q_LORA + KV_LORA + QK_ROPE:b0 + B_COLS]
    kr_grp = jnp.pad(kr, ((0, 0), (0, 0), (ROPE_LO, LANES - ROPE_LO - QK_ROPE)))
    w_in_p = jnp.concatenate([w_in[:, :, :a_end], qc, kr_grp, gate_b, w_in[:, :, A_COLS + B_COLS:]],
                             axis=-1).astype(BF16)
    zeros = jnp.zeros((depth, DECAY_LORA, A_WIDTH), F32)
    lora = jnp.concatenate([jnp.concatenate([rwkv_w2, zeros], axis=2),
                            jnp.concatenate([zeros, rwkv_a2], axis=2)], axis=1).astype(BF16)
    row = lambda p: p.reshape(depth, 1, -1)
    w_uq = mla_w_uq.reshape(depth, Q_LORA, B_HEADS, QK_DIM)
    w_uq = jnp.pad(w_uq, ((0, 0), (0, 0), (0, 0), (0, HEAD_PAD - QK_DIM))).reshape(depth, Q_LORA, QK_WIDTH)
    g_q = jnp.pad(jnp.concatenate([mla_qn_nope, mla_qn_rope], axis=1), ((0, 0), (0, HEAD_PAD - QK_DIM)))
    g_kr = jnp.pad(mla_kn_rope, ((0, 0), (ROPE_LO, LANES - ROPE_LO - QK_ROPE)))
    g_kn = jnp.pad(mla_kn_nope, ((0, 0), (0, HEAD_PAD - QK_NOPE)))
    w_ukv = mla_w_ukv.reshape(depth, KV_LORA, B_HEADS, QK_NOPE + V_DIM)
    w_uk = jnp.pad(w_ukv[..., :QK_NOPE], ((0, 0), (0, 0), (0, 0), (0, HEAD_PAD - QK_NOPE)))
    w_uv = jnp.pad(w_ukv[..., QK_NOPE:], ((0, 0), (0, 0), (0, 0), (0, HEAD_PAD - V_DIM)))
    return dict(
        w_in=w_in_p, shift_mix=shift_mix, w0=row(rwkv_w0), a0=row(rwkv_a0), lora=lora,
        k_k=row(rwkv_k_k), k_a=row(rwkv_k_a), r_k=row(rwkv_r_k), ln_w=row(rwkv_ln_w), ln_b=row(rwkv_ln_b),
        q_norm=row(mla_q_norm), w_uq=w_uq.astype(BF16), g_q=row(g_q), kv_norm=row(mla_kv_norm),
        g_kr=row(g_kr), g_kn=row(g_kn),
        w_uk=w_uk.reshape(depth, KV_LORA, QK_WIDTH).astype(BF16),
        w_uv=w_uv.reshape(depth, KV_LORA, QK_WIDTH).astype(BF16),
        w_a=w_branch_a.astype(BF16), w_b=w_branch_b.astype(BF16), w_o=w_out.astype(BF16))


def _rope_tables(t):
    half = QK_ROPE // 2
    inv = ROPE_THETA ** (-jnp.arange(0, QK_ROPE, 2, dtype=F32) / QK_ROPE)
    ang = t.astype(F32)[:, None] * inv[None, :]
    cos, sin = jnp.cos(ang), jnp.sin(ang)
    n = t.shape[0]
    ones = lambda w: jnp.ones((n, w), F32)
    zeros = lambda w: jnp.zeros((n, w), F32)
    tail = LANES - ROPE_LO - QK_ROPE
    cos_t = jnp.concatenate([ones(ROPE_LO), cos, cos, ones(tail)], axis=1)
    sin_a = jnp.concatenate([zeros(ROPE_LO), -sin, zeros(half), zeros(tail)], axis=1)
    sin_b = jnp.concatenate([zeros(ROPE_LO), zeros(half), sin, zeros(tail)], axis=1)
    return cos_t, sin_a, sin_b


def _largest_tile(n, cap, mult):
    best = mult
    for t in range(mult, min(n, cap) + 1, mult):
        if n % t == 0:
            best = t
    return best


def _layer(x, layer, wts, tabs, cfg, shift_row, s0, past=None):
    bv, lv, _ = x.shape
    bs, ls = cfg["bs"], cfg["ls"]
    seq = lambda a: a.reshape(bs, ls, a.shape[-1])
    za, zq, zgb, zg = _in_proj(x, wts["norm_w"], wts["w_in"], layer, cfg["tm"], cfg["n_pad"])
    za_s = seq(za)
    prep = _rwkv_prep(za_s, shift_row, wts, layer, cfg["tm_prep"], cfg["chunk"])
    ya, s_bd = _rwkv_chain(prep, _state_to_bd(s0, cfg["g"]), wts, layer, cfg["nb"], cfg["nc"], cfg["chunk"], cfg["g"])
    q, lat, krg = _mla_q(zq, tabs, wts, layer, cfg["tm"])
    if past is None:
        k, v = _mla_expand(lat, krg, wts, layer, cfg["tm"])
        k, v = seq(k), seq(v)
        bias, tk = cfg["bias"], cfg["tk"]
    else:
        past_lat, past_kr = past
        n_past = past_lat.shape[1]
        k_hi = n_past + ls
        lk = -(-k_hi // LANES) * LANES
        kr_grp = jnp.pad(past_kr, ((0, 0), (0, 0), (ROPE_LO, LANES - ROPE_LO - QK_ROPE)))
        pad = ((0, 0), (0, lk - k_hi), (0, 0))
        lat_all = jnp.pad(jnp.concatenate([past_lat, seq(lat)], axis=1), pad)
        krg_all = jnp.pad(jnp.concatenate([kr_grp, seq(krg)], axis=1), pad)
        k, v = _mla_expand(lat_all, krg_all, wts, layer, _largest_tile(lk, 1024, LANES))
        bias, tk = _tail_bias(cfg["tq"], lk, k_hi), lk
    yb = _attention(seq(q), k, v, seq(zgb), bias, cfg["tq"], tk, past is None)
    x_new = _out_proj(x, ya.reshape(bv, lv, A_WIDTH), yb.reshape(bv, lv, B_WIDTH), zg, wts, layer, cfg["tm"])
    return x_new, _state_from_bd(s_bd, cfg["g"]), za_s[:, -1:, :], seq(lat), seq(krg)[:, :, ROPE_LO:ROPE_LO + QK_ROPE]


def kernel(x_prompt, x_sample, state_rwkv, state_shift, cache_mla_latent, cache_mla_krope, meta_tokens, norm_w, w_in, shift_mix, rwkv_w0, rwkv_w2, rwkv_a0, rwkv_a2, rwkv_k_k, rwkv_k_a, rwkv_r_k, rwkv_ln_w, rwkv_ln_b, mla_q_norm, mla_w_uq, mla_kv_norm, mla_w_ukv, mla_qn_nope, mla_kn_nope, mla_qn_rope, mla_kn_rope, w_branch_a, w_branch_b, w_out):
    depth = w_in.shape[0]
    bp, seq_len, _ = x_prompt.shape
    bd, dec_len, _ = x_sample.shape
    past_len = cache_mla_latent.shape[2]
    assert seq_len % LANES == 0 and dec_len % 8 == 0
    wts = _prep_weights(w_in, shift_mix, rwkv_w0, rwkv_w2, rwkv_a0, rwkv_a2, rwkv_k_k, rwkv_k_a, rwkv_r_k,
                        rwkv_ln_w, rwkv_ln_b, mla_q_norm, mla_w_uq, mla_kv_norm, mla_w_ukv, mla_qn_nope,
                        mla_kn_nope, mla_qn_rope, mla_kn_rope, w_branch_a, w_branch_b, w_out)
    wts["norm_w"] = norm_w.reshape(depth, 1, D_MODEL)

    n_zero = PAD_FRONT - N_META
    lp = PAD_FRONT + seq_len
    meta = jnp.broadcast_to(meta_tokens[None].astype(F32), (bp, N_META, D_MODEL))
    xp = jnp.concatenate([jnp.zeros((bp, n_zero, D_MODEL), F32), meta, x_prompt], axis=1)
    tabs_p = _rope_tables(jnp.arange(lp, dtype=jnp.int32) - PAD_FRONT)
    tile_p = _largest_tile(lp, 512, LANES)
    cfg_p = dict(bs=bp, ls=lp, tm=tile_p, n_pad=n_zero, tm_prep=_largest_tile(lp, 256, CHUNK),
                 chunk=CHUNK, g=4, nb=4, nc=1, tq=tile_p, tk=tile_p, bias=_prompt_bias(tile_p, n_zero))
    zero_row = jnp.zeros((bp, 1, A_COLS), F32)
    zero_state = jnp.zeros((bp, A_HEADS, A_HEAD_DIM, A_HEAD_DIM), F32)

    n_rows = bd * dec_len
    xs = x_sample.reshape(1, n_rows, D_MODEL)
    t_s = past_len + (jnp.arange(n_rows, dtype=jnp.int32) % dec_len)
    tabs_s = _rope_tables(t_s)
    cfg_s = dict(bs=bd, ls=dec_len, tm=_largest_tile(n_rows, 512, 8), n_pad=0, tm_prep=dec_len,
                 chunk=dec_len, g=A_HEADS, nb=4, nc=1, tq=dec_len)

    outs_p, outs_s = [], []
    for l in range(depth):
        xp, *op = _layer(xp, l, wts, tabs_p, cfg_p, zero_row, zero_state)
        xs, *os_ = _layer(xs, l, wts, tabs_s, cfg_s, state_shift[l], state_rwkv[l],
                          past=(cache_mla_latent[l], cache_mla_krope[l]))
        outs_p.append(op)
        outs_s.append(os_)
    stack = lambda outs, i: jnp.stack([o[i] for o in outs])
    first = PAD_FRONT - N_META
    return (xp[:, PAD_FRONT:], xs.reshape(bd, dec_len, D_MODEL),
            stack(outs_p, 0), stack(outs_p, 1), stack(outs_p, 2)[:, :, first:], stack(outs_p, 3)[:, :, first:],
            stack(outs_s, 0), stack(outs_s, 1), stack(outs_s, 2), stack(outs_s, 3))
```

```python
import functools

import numpy as np
import jax
import jax.numpy as jnp
from jax import lax
from jax.experimental import pallas as pl
from jax.experimental.pallas import tpu as pltpu

F32 = jnp.float32
BF16 = jnp.bfloat16

D_MODEL = 1024
CHUNK = 64
N_META = 16
A_HEADS = 8
A_HEAD_DIM = 64
A_WIDTH = A_HEADS * A_HEAD_DIM
DECAY_LORA = 64
ICLR_LORA = 64
B_HEADS = 8
QK_NOPE = 64
QK_ROPE = 32
QK_DIM = QK_NOPE + QK_ROPE
V_DIM = 64
B_WIDTH = B_HEADS * V_DIM
Q_LORA = 384
KV_LORA = 256
ROPE_THETA = 10000.0
NORM_EPS = 1e-6
GN_EPS = 64e-5
NEG_INF = -1e30
A_COLS = 4 * A_WIDTH + DECAY_LORA + ICLR_LORA
B_COLS = Q_LORA + KV_LORA + QK_ROPE + B_WIDTH
G_COLS = 2 * D_MODEL

LANES = 128
HEAD_PAD = LANES
QK_WIDTH = B_HEADS * HEAD_PAD
ZQ_COLS = Q_LORA + KV_LORA + LANES
ROPE_LO = QK_NOPE
PAD_FRONT = 128
VMEM_LIMIT = 56 * 1024 * 1024
BASE_BLOCK = 8


def _cparams(sem):
    return pltpu.CompilerParams(dimension_semantics=sem, vmem_limit_bytes=VMEM_LIMIT)


def _mm(a, b):
    return jnp.dot(a.astype(BF16), b.astype(BF16), preferred_element_type=F32)


def _mm_nt(a, b):
    return lax.dot_general(a.astype(BF16), b.astype(BF16), (((1,), (1,)), ((), ())),
                           preferred_element_type=F32)


def _mm_tn(a, b):
    return lax.dot_general(a.astype(BF16), b.astype(BF16), (((0,), (0,)), ((), ())),
                           preferred_element_type=F32)


def _split2(x):
    hi = x.astype(BF16)
    return hi, (x - hi.astype(F32)).astype(BF16)


def _mm_x_exact(x, e):
    hi, lo = _split2(x)
    return jnp.dot(hi, e, preferred_element_type=F32) + jnp.dot(lo, e, preferred_element_type=F32)


def _mm_exact_x(e, x):
    hi, lo = _split2(x)
    return jnp.dot(e, hi, preferred_element_type=F32) + jnp.dot(e, lo, preferred_element_type=F32)


def _sigmoid(x):
    return 1.0 / (1.0 + jnp.exp(-x))


def _in_proj_kernel(x_ref, nw_ref, w_ref, za_ref, zq_ref, zgb_ref, zg_ref, *, tm, n_pad):
    x = x_ref[0]
    ms = jnp.mean(x * x, axis=-1, keepdims=True)
    h = x * lax.rsqrt(ms + NORM_EPS) * nw_ref[0]
    if n_pad:
        row = pl.program_id(1) * tm + lax.broadcasted_iota(jnp.int32, (tm, 1), 0)
        h = jnp.where(row >= n_pad, h, 0.0)
    hb = h.astype(BF16)
    o = 0
    for ref, width in ((za_ref, A_COLS), (zq_ref, ZQ_COLS), (zgb_ref, B_WIDTH), (zg_ref, G_COLS)):
        ref[0] = jnp.dot(hb, w_ref[0, :, o:o + width], preferred_element_type=F32)
        o += width


def _in_proj(x, norm_w, w_in_p, layer, tm, n_pad):
    bv, lv, _ = x.shape
    cols = w_in_p.shape[-1]
    row = lambda w: pl.BlockSpec((1, tm, w), lambda b, j: (b, j, 0))
    return pl.pallas_call(
        functools.partial(_in_proj_kernel, tm=tm, n_pad=n_pad),
        grid=(bv, lv // tm),
        in_specs=[row(D_MODEL),
                  pl.BlockSpec((1, 1, D_MODEL), lambda b, j: (layer, 0, 0)),
                  pl.BlockSpec((1, D_MODEL, cols), lambda b, j: (layer, 0, 0))],
        out_specs=[row(A_COLS), row(ZQ_COLS), row(B_WIDTH), row(G_COLS)],
        out_shape=[jax.ShapeDtypeStruct((bv, lv, w), F32) for w in (A_COLS, ZQ_COLS, B_WIDTH, G_COLS)],
        compiler_params=_cparams(("parallel", "parallel")),
        name="in_proj",
    )(x, norm_w, w_in_p)


def _rwkv_prep_kernel(za_ref, zap_ref, sh_ref, mix_ref, w0_ref, a0_ref, lora_ref, kk_ref, ka_ref,
                      rk_ref, e_ref, tri_ref,
                      at_ref, rt_ref, bt_ref, kt_ref, vb_ref, pc_ref, bonus_ref, sg_ref, *, tm):
    j = pl.program_id(1)
    a = za_ref[0]
    first = jnp.where(j == 0, sh_ref[0], zap_ref[0, 7:8, :])
    rowid = lax.broadcasted_iota(jnp.int32, (tm, 1), 0)
    prev = jnp.where(rowid == 0, first, pltpu.roll(a, 1, 0))
    xs = a * mix_ref[0, 0:1, :] + prev * mix_ref[0, 1:2, :]
    r = xs[:, 0:A_WIDTH]
    k = xs[:, A_WIDTH:2 * A_WIDTH]
    v = xs[:, 2 * A_WIDTH:3 * A_WIDTH]
    gate = xs[:, 3 * A_WIDTH:4 * A_WIDTH]
    lr = xs[:, 4 * A_WIDTH:A_COLS]
    lane = lax.broadcasted_iota(jnp.int32, (1, DECAY_LORA + ICLR_LORA), 1)
    lr = jnp.where(lane < DECAY_LORA, jnp.tanh(lr), lr)
    lo = _mm(lr, lora_ref[0])
    wlin = w0_ref[0] + lo[:, 0:A_WIDTH]
    alin = a0_ref[0] + lo[:, A_WIDTH:2 * A_WIDTH]
    softplus_neg = jnp.maximum(-wlin, 0.0) + jnp.log1p(jnp.exp(-jnp.abs(wlin)))
    logdec = -jnp.exp(-softplus_neg - 0.5)
    av = _sigmoid(alin)
    e = e_ref[...]
    kk = k * kk_ref[0]
    kk = kk * lax.rsqrt(jnp.maximum(_mm_x_exact(kk * kk, e), 1e-24))
    kmod = k * (1.0 + (av - 1.0) * ka_ref[0])
    sums = _mm_exact_x(tri_ref[...], logdec)
    cum, tot = sums[:tm], sums[tm:]
    pin = jnp.exp(cum)
    pex = jnp.exp(cum - logdec)
    pinv = jnp.exp(-cum)
    at_ref[0] = (-kk * pex).astype(BF16)
    rt_ref[0] = (r * pin).astype(BF16)
    bt_ref[0] = (kk * av * pinv).astype(BF16)
    kt_ref[0] = (kmod * pinv).astype(BF16)
    vb_ref[0] = v.astype(BF16)
    pc_ref[0] = jnp.exp(tot)
    bonus_ref[0] = _mm_x_exact(r * kmod * rk_ref[0], e) * v
    sg_ref[0] = gate * _sigmoid(gate)


def _rwkv_prep(za, shift_row, wts, layer, tm, chunk):
    bs, ls, _ = za.shape
    ids = np.arange(tm)
    same = (ids[:, None] // chunk) == (ids[None, :] // chunk)
    tri = same & (ids[None, :] <= ids[:, None])
    e = (np.arange(A_WIDTH)[:, None] // A_HEAD_DIM) == (np.arange(A_WIDTH)[None, :] // A_HEAD_DIM)
    row = lambda w: pl.BlockSpec((1, tm, w), lambda b, j: (b, j, 0))
    par = lambda r, c: pl.BlockSpec((1, r, c), lambda b, j: (layer, 0, 0))
    const = lambda r, c: pl.BlockSpec((r, c), lambda b, j: (0, 0))
    nsub = tm // 8
    outs = [jax.ShapeDtypeStruct((bs, ls, A_WIDTH), BF16)] * 5 + [jax.ShapeDtypeStruct((bs, ls, A_WIDTH), F32)] * 3
    return pl.pallas_call(
        functools.partial(_rwkv_prep_kernel, tm=tm),
        grid=(bs, ls // tm),
        in_specs=[row(A_COLS),
                  pl.BlockSpec((1, 8, A_COLS), lambda b, j: (b, jnp.maximum(j * nsub - 1, 0), 0)),
                  pl.BlockSpec((1, 1, A_COLS), lambda b, j: (b, 0, 0)),
                  par(2, A_COLS), par(1, A_WIDTH), par(1, A_WIDTH),
                  par(DECAY_LORA + ICLR_LORA, 2 * A_WIDTH),
                  par(1, A_WIDTH), par(1, A_WIDTH), par(1, A_WIDTH),
                  const(A_WIDTH, A_WIDTH), const(2 * tm, tm)],
        out_specs=[row(A_WIDTH)] * 8,
        out_shape=outs,
        compiler_params=_cparams(("parallel", "parallel")),
        name="rwkv_prep",
    )(za, za, shift_row, wts["shift_mix"], wts["w0"], wts["a0"], wts["lora"], wts["k_k"], wts["k_a"],
      wts["r_k"], jnp.asarray(e, BF16), jnp.asarray(np.concatenate([tri, same], axis=0), BF16))


def _chain_masks(chunk, g):
    gc, w = g * chunk, g * A_HEAD_DIM
    t = np.arange(chunk)[:, None]
    s = np.arange(gc)[None, :] % chunk
    wide = {"strict": s < t, "incl": s <= t, "eye": s == t,
            "base": (s // BASE_BLOCK == t // BASE_BLOCK) & (s < t)}
    levels = []
    sz = BASE_BLOCK
    while sz < chunk:
        levels.append((s // (2 * sz) == t // (2 * sz)) & (s // sz != t // sz) & (s < t))
        sz *= 2
    rows = np.arange(gc)[:, None] // chunk
    bd_nat = rows == (np.arange(w)[None, :] // A_HEAD_DIM)
    bd_wide = rows == (np.arange(gc)[None, :] // chunk)
    bd_state = (np.arange(w)[:, None] // A_HEAD_DIM) == (np.arange(w)[None, :] // A_HEAD_DIM)
    f = lambda m: jnp.asarray(m, F32)
    b = lambda m: jnp.asarray(m, BF16)
    lev = np.stack(levels) if levels else np.zeros((1, chunk, gc), bool)
    return dict(strict=f(wide["strict"]), incl=f(wide["incl"]), eye=f(wide["eye"]), base=f(wide["base"]),
                lev=f(lev), bd_nat=b(bd_nat), bd_wide=b(bd_wide), bd_state=f(bd_state)), len(levels)


def _rwkv_chain_kernel(at_ref, rt_ref, bt_ref, kt_ref, vb_ref, pc_ref, bonus_ref, sg_ref, s0_ref,
                       lnw_ref, lnb_ref, e_ref, strict_ref, incl_ref, eye_ref, base_ref, lev_ref,
                       bdn_ref, bdw_ref, bds_ref, rep_ref, rept_ref, ya_ref, sout_ref, s_ref, *,
                       nb, nc, chunk, g, n_lev):
    c = chunk
    gc, w = g * c, g * A_HEAD_DIM
    n_groups = A_HEADS // g

    @pl.when(pl.program_id(1) == 0)
    def _():
        for b in range(nb):
            for grp in range(n_groups):
                stacked = s0_ref[b, grp * g:(grp + 1) * g].reshape(w, A_HEAD_DIM)
                s_ref[b, grp] = _mm_x_exact(stacked, rep_ref[...]) * bds_ref[...]

    strict, incl, eye, base = strict_ref[...], incl_ref[...], eye_ref[...], base_ref[...]
    bdn, bdw, bds = bdn_ref[...], bdw_ref[...], bds_ref[...]

    def bd(x, mask):
        return jnp.concatenate([x.astype(BF16)] * g, axis=0) * mask

    def wide_mm(x, y):
        return jnp.dot(x.astype(BF16), bd(y, bdw), preferred_element_type=F32)

    chains = [(b, grp) for b in range(nb) for grp in range(n_groups)]
    lanes = [slice(grp * w, (grp + 1) * w) for _, grp in chains]
    each = lambda fn, *lists: [fn(*args) for args in zip(*lists)]
    dot = lambda x, y: jnp.dot(x.astype(BF16), y, preferred_element_type=F32)
    cat0 = lambda x, y: jnp.concatenate([x, y], axis=0)
    states = [s_ref[b, grp] for b, grp in chains]
    y_chunks = [[] for _ in chains]
    for ci in range(nc):
        rows = slice(ci * c, (ci + 1) * c)
        load = lambda ref: [ref[b, rows, sl] for (b, _), sl in zip(chains, lanes)]
        at, rt, bt, kt, vb = (load(ref) for ref in (at_ref, rt_ref, bt_ref, kt_ref, vb_ref))
        ar = each(cat0, at, rt)
        gm = each(lambda a, b_, k_: _mm_nt(a, cat0(bd(b_, bdn), bd(k_, bdn))), ar, bt, kt)
        m_ab = [g_[:c, :gc] * strict for g_ in gm]
        d1 = [m * base for m in m_ab]
        d2 = each(wide_mm, d1, d1)
        t = [eye + d for d in d1]
        t = each(lambda t_, d: t_ + wide_mm(t_, d), t, d2)
        d4 = each(wide_mm, d2, d2)
        t = each(lambda t_, d: t_ + wide_mm(t_, d), t, d4)
        for lv in range(n_lev):
            off = each(lambda m, t_: wide_mm(m * lev_ref[lv], t_), m_ab, t)
            t = each(lambda t_, o: t_ + wide_mm(t_, o), t, off)
        vbd = [bd(v, bdn) for v in vb]
        akv = each(lambda g_, v: dot(g_[:c, gc:] * strict, v), gm, vbd)
        ah = each(_mm_nt, ar, states)
        u = each(lambda t_, a, k_: dot(t_, bd(a[:c] + k_, bdn)).astype(BF16), t, ah, akv)
        y = each(lambda a, g_, u_, v: a[c:] + dot(g_[c:] * jnp.concatenate([incl, incl], axis=1),
                                                  cat0(bd(u_, bdn), v)), ah, gm, u, vbd)
        upd = each(lambda u_, v, b_, k_: _mm_tn(cat0(u_, v), cat0(b_, k_)), u, vb, bt, kt)
        states = [(s + up * bds) * pc_ref[b, ci * c:ci * c + 1, sl]
                  for s, up, (b, _), sl in zip(states, upd, chains, lanes)]
        for yc, y_ in zip(y_chunks, y):
            yc.append(y_)
    for (b, grp), s in zip(chains, states):
        s_ref[b, grp] = s
    seqs = [yc[0] if nc == 1 else jnp.concatenate(yc, axis=0) for yc in y_chunks]
    ys = [seqs[b * n_groups] if n_groups == 1 else jnp.concatenate(seqs[b * n_groups:(b + 1) * n_groups], axis=1)
          for b in range(nb)]
    y = ys[0] if nb == 1 else jnp.concatenate(ys, axis=0)
    e = e_ref[...]
    inv_n = 1.0 / A_HEAD_DIM
    mu = _mm_x_exact(y, e) * inv_n
    d = y - mu
    var = _mm_x_exact(d * d, e) * inv_n
    yn = d * lax.rsqrt(var + GN_EPS) * lnw_ref[0] + lnb_ref[0]
    bonus = bonus_ref[...].reshape(nb * nc * c, A_WIDTH)
    sg = sg_ref[...].reshape(nb * nc * c, A_WIDTH)
    ya_ref[...] = ((yn + bonus) * sg).astype(BF16).reshape(nb, nc * c, A_WIDTH)

    @pl.when(pl.program_id(1) == pl.num_programs(1) - 1)
    def _():
        for (b, grp), s in zip(chains, states):
            blocks = _mm_x_exact(s, rept_ref[...])
            sout_ref[b, grp * g:(grp + 1) * g] = blocks.reshape(g, A_HEAD_DIM, A_HEAD_DIM)


def _rwkv_chain(prep, s0, wts, layer, nb, nc, chunk, g):
    at, rt, bt, kt, vb, pc, bonus, sg = prep
    bs, ls, _ = at.shape
    gc, w = g * chunk, g * A_HEAD_DIM
    n_groups = A_HEADS // g
    masks, n_lev = _chain_masks(chunk, g)
    e = (np.arange(A_WIDTH)[:, None] // A_HEAD_DIM) == (np.arange(A_WIDTH)[None, :] // A_HEAD_DIM)
    row = pl.BlockSpec((nb, nc * chunk, A_WIDTH), lambda b, j: (b, j, 0))
    par = pl.BlockSpec((1, 1, A_WIDTH), lambda b, j: (layer, 0, 0))
    state = pl.BlockSpec((nb, A_HEADS, A_HEAD_DIM, A_HEAD_DIM), lambda b, j: (b, 0, 0, 0))
    const = lambda shape: pl.BlockSpec(shape, lambda b, j: (0,) * len(shape))
    rep = np.tile(np.eye(A_HEAD_DIM), (1, g))
    return pl.pallas_call(
        functools.partial(_rwkv_chain_kernel, nb=nb, nc=nc, chunk=chunk, g=g, n_lev=n_lev),
        grid=(bs // nb, ls // (nc * chunk)),
        in_specs=[row] * 8 + [state, par, par, const((A_WIDTH, A_WIDTH)),
                              const((chunk, gc)), const((chunk, gc)), const((chunk, gc)), const((chunk, gc)),
                              const(tuple(masks["lev"].shape)),
                              const((gc, w)), const((gc, gc)), const((w, w)),
                              const((A_HEAD_DIM, w)), const((w, A_HEAD_DIM))],
        out_specs=[row, state],
        out_shape=[jax.ShapeDtypeStruct((bs, ls, A_WIDTH), BF16),
                   jax.ShapeDtypeStruct((bs, A_HEADS, A_HEAD_DIM, A_HEAD_DIM), F32)],
        scratch_shapes=[pltpu.VMEM((nb, n_groups, w, w), F32)],
        compiler_params=_cparams(("parallel", "arbitrary")),
        name="rwkv_chain",
    )(at, rt, bt, kt, vb, pc, bonus, sg, s0, wts["ln_w"], wts["ln_b"], jnp.asarray(e, BF16),
      masks["strict"], masks["incl"], masks["eye"], masks["base"], masks["lev"],
      masks["bd_nat"], masks["bd_wide"], masks["bd_state"], jnp.asarray(rep, BF16), jnp.asarray(rep.T, BF16))


def _segment_tables(bounds):
    sel = np.zeros((QK_WIDTH, LANES), np.float32)
    inv_len = np.ones((1, LANES), np.float32)
    for col, (lo, hi) in enumerate(bounds):
        sel[lo:hi, col] = 1.0
        inv_len[0, col] = 1.0 / (hi - lo)
    return jnp.asarray(sel, BF16), jnp.asarray(sel.T, BF16), jnp.asarray(inv_len, F32)


def _segment_rms_scale(x, sel_ref, selt_ref, ilen_ref):
    ss = _mm_x_exact(x * x, sel_ref[...])
    return _mm_x_exact(lax.rsqrt(ss * ilen_ref[...] + NORM_EPS), selt_ref[...])


def _mla_q_kernel(zq_ref, cos_ref, sin_ref, qn_ref, wuq_ref, wrot_ref, gq_ref, gqr_ref, kvn_ref, gk_ref,
                  sel_ref, selt_ref, ilen_ref, q_ref, lat_ref, krg_ref):
    zq = zq_ref[0]
    qc = zq[:, 0:Q_LORA]
    ckv = zq[:, Q_LORA:Q_LORA + KV_LORA]
    kr = zq[:, Q_LORA + KV_LORA:ZQ_COLS]
    rms = lambda x, n: x * lax.rsqrt(jnp.sum(x * x, axis=-1, keepdims=True) * (1.0 / n) + NORM_EPS)
    qn = (rms(qc, Q_LORA) * qn_ref[0]).astype(BF16)
    qf = jnp.dot(qn, wuq_ref[0], preferred_element_type=F32)
    qp = jnp.dot(qn, wrot_ref[0], preferred_element_type=F32)
    cos, sin = cos_ref[...], sin_ref[...]
    cos_all = jnp.concatenate([cos] * B_HEADS, axis=1)
    sin_all = jnp.concatenate([sin] * B_HEADS, axis=1)
    scale = _segment_rms_scale(qf, sel_ref, selt_ref, ilen_ref)
    q_ref[0] = (scale * (qf * gq_ref[0] * cos_all + qp * gqr_ref[0] * sin_all)).astype(BF16)
    lat_ref[0] = rms(ckv, KV_LORA) * kvn_ref[0]
    krn = rms(kr, QK_ROPE) * gk_ref[0]
    half = QK_ROPE // 2
    lane = lax.broadcasted_iota(jnp.int32, (1, LANES), 1)
    partner = jnp.where(lane < ROPE_LO + half, pltpu.roll(krn, LANES - half, 1), pltpu.roll(krn, half, 1))
    krg_ref[0] = krn * cos + partner * sin


def _mla_q(zq, tabs, wts, layer, tm):
    bv, lv, _ = zq.shape
    row = lambda w: pl.BlockSpec((1, tm, w), lambda b, j: (b, j, 0))
    tab = pl.BlockSpec((tm, LANES), lambda b, j: (j, 0))
    par = lambda r, c: pl.BlockSpec((1, r, c), lambda b, j: (layer, 0, 0))
    const = lambda a: pl.BlockSpec(a.shape, lambda b, j: (0, 0))
    segs = _segment_tables([b for h in range(B_HEADS) for b in
                            ((h * HEAD_PAD, h * HEAD_PAD + QK_NOPE), (h * HEAD_PAD + QK_NOPE, h * HEAD_PAD + QK_DIM))])
    return pl.pallas_call(
        _mla_q_kernel,
        grid=(bv, lv // tm),
        in_specs=[row(ZQ_COLS), tab, tab, par(1, Q_LORA), par(Q_LORA, QK_WIDTH), par(Q_LORA, QK_WIDTH),
                  par(1, QK_WIDTH), par(1, QK_WIDTH), par(1, KV_LORA), par(1, LANES)] + [const(a) for a in segs],
        out_specs=[row(QK_WIDTH), row(KV_LORA), row(LANES)],
        out_shape=[jax.ShapeDtypeStruct((bv, lv, QK_WIDTH), BF16),
                   jax.ShapeDtypeStruct((bv, lv, KV_LORA), F32),
                   jax.ShapeDtypeStruct((bv, lv, LANES), F32)],
        compiler_params=_cparams(("parallel", "parallel")),
        name="mla_q",
    )(zq, tabs[0], tabs[1], wts["q_norm"], wts["w_uq"], wts["w_uq_rot"], wts["g_q"], wts["g_q_rot"],
      wts["kv_norm"], wts["g_kr"], *segs)


def _mla_expand_kernel(lat_ref, krg_ref, wk_ref, wv_ref, gk_ref, sel_ref, selt_ref, ilen_ref, k_ref, v_ref):
    lat = lat_ref[0].astype(BF16)
    kf = jnp.dot(lat, wk_ref[0], preferred_element_type=F32)
    scale = _segment_rms_scale(kf, sel_ref, selt_ref, ilen_ref)
    k_ref[0] = (kf * scale * gk_ref[0] + jnp.concatenate([krg_ref[0]] * B_HEADS, axis=1)).astype(BF16)
    lane = lax.broadcasted_iota(jnp.int32, (1, QK_WIDTH), 1)
    ones_col = jnp.where(lane % HEAD_PAD == V_DIM, 1.0, 0.0)
    v_ref[0] = (jnp.dot(lat, wv_ref[0], preferred_element_type=F32) + ones_col).astype(BF16)


def _mla_expand(lat, krg, wts, layer, tm, src_layer=None):
    lv = lat.shape[1]
    bv = 1 if src_layer is not None else lat.shape[0]
    lead = (lambda b: src_layer) if src_layer is not None else (lambda b: b)
    src = lambda w: pl.BlockSpec((1, tm, w), lambda b, j: (lead(b), j, 0))
    row = lambda w: pl.BlockSpec((1, tm, w), lambda b, j: (b, j, 0))
    par = lambda r, c: pl.BlockSpec((1, r, c), lambda b, j: (layer, 0, 0))
    const = lambda a: pl.BlockSpec(a.shape, lambda b, j: (0, 0))
    segs = _segment_tables([(h * HEAD_PAD, h * HEAD_PAD + QK_NOPE) for h in range(B_HEADS)])
    return pl.pallas_call(
        _mla_expand_kernel,
        grid=(bv, lv // tm),
        in_specs=[src(KV_LORA), src(LANES), par(KV_LORA, QK_WIDTH), par(KV_LORA, QK_WIDTH), par(1, QK_WIDTH)]
                 + [const(a) for a in segs],
        out_specs=[row(QK_WIDTH), row(QK_WIDTH)],
        out_shape=[jax.ShapeDtypeStruct((bv, lv, QK_WIDTH), BF16),
                   jax.ShapeDtypeStruct((bv, lv, QK_WIDTH), BF16)],
        compiler_params=_cparams(("parallel", "parallel")),
        name="mla_expand",
    )(lat, krg, wts["w_uk"], wts["w_uv"], wts["g_kn"], *segs)


def _attn_block(q_ref, k_ref, v_ref, start, tk, bias, m_ref, acc_ref):
    rows = pl.ds(start, tk)

    def scores(h):
        lanes = slice(h * HEAD_PAD, (h + 1) * HEAD_PAD)
        s = _mm_nt(q_ref[0, :, lanes], k_ref[0, rows, lanes])
        return s if bias is None else s + bias

    s_next = scores(0)
    for h in range(B_HEADS):
        s = s_next
        if h + 1 < B_HEADS:
            s_next = scores(h + 1)
        m_old = m_ref[h]
        m_new = jnp.maximum(m_old, jnp.max(s, axis=-1, keepdims=True))
        m_wide = m_new[:, :tk] if tk < LANES else jnp.concatenate([m_new] * (tk // LANES), axis=1)
        p = jnp.exp2(s - m_wide)
        pv = jnp.dot(p.astype(BF16), v_ref[0, rows, h * HEAD_PAD:(h + 1) * HEAD_PAD],
                     preferred_element_type=F32)
        acc_ref[h] = jnp.exp2(m_old - m_new) * acc_ref[h] + pv
        m_ref[h] = m_new


def _attn_finish(gb_ref, o_ref, acc_ref):
    outs = []
    for h in range(B_HEADS):
        acc = acc_ref[h]
        outs.append(acc[:, 0:V_DIM] / acc[:, V_DIM:V_DIM + 1])
    gb = gb_ref[0]
    o_ref[0] = (jnp.concatenate(outs, axis=1) * (gb * _sigmoid(gb))).astype(BF16)


def _attn_causal_kernel(q_ref, k_ref, v_ref, gb_ref, bias_ref, o_ref, m_ref, acc_ref, *, t):
    qi = pl.program_id(1)
    m_ref[...] = jnp.full(m_ref.shape, NEG_INF, F32)
    acc_ref[...] = jnp.zeros(acc_ref.shape, F32)
    block = functools.partial(_attn_block, q_ref, k_ref, v_ref, tk=t, m_ref=m_ref, acc_ref=acc_ref)
    block(start=0, bias=bias_ref[jnp.where(qi == 0, 2, 1)])

    def kv_step(c, carry):
        block(start=pl.multiple_of(c * t, t), bias=None)
        return carry

    lax.fori_loop(1, qi, kv_step, 0)

    @pl.when(qi > 0)
    def _():
        block(start=pl.multiple_of(qi * t, t), bias=bias_ref[0])

    _attn_finish(gb_ref, o_ref, acc_ref)


def _attention_causal(q, k, v, gb, bias, t):
    bs, lq, _ = q.shape
    whole = lambda: pl.BlockSpec((1, lq, QK_WIDTH), lambda b, i: (b, 0, 0), pipeline_mode=pl.Buffered(1))
    return pl.pallas_call(
        functools.partial(_attn_causal_kernel, t=t),
        grid=(bs, lq // t),
        in_specs=[pl.BlockSpec((1, t, QK_WIDTH), lambda b, i: (b, i, 0)),
                  whole(), whole(),
                  pl.BlockSpec((1, t, B_WIDTH), lambda b, i: (b, i, 0)),
                  pl.BlockSpec(bias.shape, lambda b, i: (0, 0, 0))],
        out_specs=pl.BlockSpec((1, t, B_WIDTH), lambda b, i: (b, i, 0)),
        out_shape=jax.ShapeDtypeStruct((bs, lq, B_WIDTH), BF16),
        scratch_shapes=[pltpu.VMEM((B_HEADS, t, LANES), F32), pltpu.VMEM((B_HEADS, t, HEAD_PAD), F32)],
        compiler_params=_cparams(("parallel", "arbitrary")),
        name="attention",
    )(q, k, v, gb, bias)


def _prompt_bias(t, n_invisible):
    i = np.arange(t)
    diag = np.where((i[None, :] // CHUNK) <= (i[:, None] // CHUNK), 0.0, NEG_INF)
    first = np.broadcast_to(np.where(i[None, :] >= n_invisible, 0.0, NEG_INF), (t, t))
    return jnp.asarray(np.stack([diag, first, np.minimum(diag, first)]), F32)


def _attn_cached_kernel(q_ref, kp_ref, vp_ref, kn_ref, vn_ref, gb_ref, o_ref, m_ref, acc_ref):
    m_ref[...] = jnp.full(m_ref.shape, NEG_INF, F32)
    acc_ref[...] = jnp.zeros(acc_ref.shape, F32)
    _attn_block(q_ref, kp_ref, vp_ref, 0, kp_ref.shape[1], None, m_ref, acc_ref)
    _attn_block(q_ref, kn_ref, vn_ref, 0, kn_ref.shape[1], None, m_ref, acc_ref)
    _attn_finish(gb_ref, o_ref, acc_ref)


def _attention_cached(q, k_past, v_past, k_new, v_new, gb):
    bs, lq, _ = q.shape
    seq = lambda n, w: pl.BlockSpec((1, n, w), lambda b: (b, 0, 0))
    n_past = k_past.shape[1]
    return pl.pallas_call(
        _attn_cached_kernel,
        grid=(bs,),
        in_specs=[seq(lq, QK_WIDTH), seq(n_past, QK_WIDTH), seq(n_past, QK_WIDTH), seq(lq, QK_WIDTH),
                  seq(lq, QK_WIDTH), seq(lq, B_WIDTH)],
        out_specs=seq(lq, B_WIDTH),
        out_shape=jax.ShapeDtypeStruct((bs, lq, B_WIDTH), BF16),
        scratch_shapes=[pltpu.VMEM((B_HEADS, lq, LANES), F32), pltpu.VMEM((B_HEADS, lq, HEAD_PAD), F32)],
        compiler_params=_cparams(("parallel",)),
        name="attention_cached",
    )(q, k_past, v_past, k_new, v_new, gb)


def _out_proj_kernel(x_ref, ya_ref, yb_ref, zg_ref, wa_ref, wb_ref, wo_ref, o_ref):
    zg = zg_ref[0]
    ga = _sigmoid(zg[:, 0:D_MODEL])
    gb = _sigmoid(zg[:, D_MODEL:G_COLS])
    t = (ga * jnp.dot(ya_ref[0], wa_ref[0], preferred_element_type=F32)
         + gb * jnp.dot(yb_ref[0], wb_ref[0], preferred_element_type=F32))
    o_ref[0] = x_ref[0] + _mm(t, wo_ref[0])


def _out_proj(x, ya, yb, zg, wts, layer, tm):
    bv, lv, _ = x.shape
    row = lambda w: pl.BlockSpec((1, tm, w), lambda b, j: (b, j, 0))
    par = lambda r, c: pl.BlockSpec((1, r, c), lambda b, j: (layer, 0, 0))
    return pl.pallas_call(
        _out_proj_kernel,
        grid=(bv, lv // tm),
        in_specs=[row(D_MODEL), row(A_WIDTH), row(B_WIDTH), row(G_COLS),
                  par(A_WIDTH, D_MODEL), par(B_WIDTH, D_MODEL), par(D_MODEL, D_MODEL)],
        out_specs=row(D_MODEL),
        out_shape=jax.ShapeDtypeStruct((bv, lv, D_MODEL), F32),
        compiler_params=_cparams(("parallel", "parallel")),
        name="out_proj",
    )(x, ya, yb, zg, wts["w_a"], wts["w_b"], wts["w_o"])


def _prep_weights(w_in, shift_mix, rwkv_w0, rwkv_w2, rwkv_a0, rwkv_a2, rwkv_k_k, rwkv_k_a, rwkv_r_k,
                  rwkv_ln_w, rwkv_ln_b, mla_q_norm, mla_w_uq, mla_kv_norm, mla_w_ukv, mla_qn_nope,
                  mla_kn_nope, mla_qn_rope, mla_kn_rope, w_branch_a, w_branch_b, w_out):
    depth = w_in.shape[0]
    a_end, b0 = A_COLS, A_COLS
    qc = w_in[:, :, b0:b0 + Q_LORA + KV_LORA]
    kr = w_in[:, :, b0 + Q_LORA + KV_LORA:b0 + Q_LORA + KV_LORA + QK_ROPE]
    gate_b = w_in[:, :, b0 + Q_LORA + KV_LORA + QK_ROPE:b0 + B_COLS]
    kr_grp = jnp.pad(kr, ((0, 0), (0, 0), (ROPE_LO, LANES - ROPE_LO - QK_ROPE)))
    w_in_p = jnp.concatenate([w_in[:, :, :a_end], qc, kr_grp, gate_b, w_in[:, :, A_COLS + B_COLS:]],
                             axis=-1).astype(BF16)
    zeros = jnp.zeros((depth, DECAY_LORA, A_WIDTH), F32)
    lora = jnp.concatenate([jnp.concatenate([rwkv_w2, zeros], axis=2),
                            jnp.concatenate([zeros, rwkv_a2], axis=2)], axis=1).astype(BF16)
    row = lambda p: p.reshape(depth, 1, -1)
    w_uq = mla_w_uq.reshape(depth, Q_LORA, B_HEADS, QK_DIM)
    w_uq = jnp.pad(w_uq, ((0, 0), (0, 0), (0, 0), (0, HEAD_PAD - QK_DIM)))
    g_q = jnp.pad(jnp.concatenate([mla_qn_nope, mla_qn_rope], axis=1), ((0, 0), (0, HEAD_PAD - QK_DIM)))
    half = QK_ROPE // 2

    def partner(a):
        lo, hi = a[..., ROPE_LO:ROPE_LO + half], a[..., ROPE_LO + half:ROPE_LO + QK_ROPE]
        zero = jnp.zeros_like(a[..., :ROPE_LO])
        return jnp.concatenate([zero, hi, lo, zero[..., :HEAD_PAD - ROPE_LO - QK_ROPE]], axis=-1)

    q_scale = QK_DIM ** -0.5 * float(np.log2(np.e))
    heads = lambda g: jnp.tile(g, (1, B_HEADS))
    w_uq_rot = partner(w_uq).reshape(depth, Q_LORA, QK_WIDTH)
    w_uq = w_uq.reshape(depth, Q_LORA, QK_WIDTH)
    g_q_rot = heads(partner(g_q)) * q_scale
    g_q = heads(g_q) * q_scale
    g_kr = jnp.pad(mla_kn_rope, ((0, 0), (ROPE_LO, LANES - ROPE_LO - QK_ROPE)))
    g_kn = heads(jnp.pad(mla_kn_nope, ((0, 0), (0, HEAD_PAD - QK_NOPE))))
    w_ukv = mla_w_ukv.reshape(depth, KV_LORA, B_HEADS, QK_NOPE + V_DIM)
    w_uk = jnp.pad(w_ukv[..., :QK_NOPE], ((0, 0), (0, 0), (0, 0), (0, HEAD_PAD - QK_NOPE)))
    w_uv = jnp.pad(w_ukv[..., QK_NOPE:], ((0, 0), (0, 0), (0, 0), (0, HEAD_PAD - V_DIM)))
    return dict(
        w_in=w_in_p, shift_mix=shift_mix, w0=row(rwkv_w0), a0=row(rwkv_a0), lora=lora,
        k_k=row(rwkv_k_k), k_a=row(rwkv_k_a), r_k=row(rwkv_r_k), ln_w=row(rwkv_ln_w), ln_b=row(rwkv_ln_b),
        q_norm=row(mla_q_norm), w_uq=w_uq.astype(BF16), w_uq_rot=w_uq_rot.astype(BF16), g_q=row(g_q),
        g_q_rot=row(g_q_rot), kv_norm=row(mla_kv_norm),
        g_kr=row(g_kr), g_kn=row(g_kn),
        w_uk=w_uk.reshape(depth, KV_LORA, QK_WIDTH).astype(BF16),
        w_uv=w_uv.reshape(depth, KV_LORA, QK_WIDTH).astype(BF16),
        w_a=w_branch_a.astype(BF16), w_b=w_branch_b.astype(BF16), w_o=w_out.astype(BF16))


def _rope_tables(t):
    half = QK_ROPE // 2
    inv = ROPE_THETA ** (-jnp.arange(0, QK_ROPE, 2, dtype=F32) / QK_ROPE)
    ang = t.astype(F32)[:, None] * inv[None, :]
    cos, sin = jnp.cos(ang), jnp.sin(ang)
    n = t.shape[0]
    ones = lambda w: jnp.ones((n, w), F32)
    zeros = lambda w: jnp.zeros((n, w), F32)
    tail = LANES - ROPE_LO - QK_ROPE
    cos_t = jnp.concatenate([ones(ROPE_LO), cos, cos, ones(tail)], axis=1)
    sin_t = jnp.concatenate([zeros(ROPE_LO), -sin, sin, zeros(tail)], axis=1)
    return cos_t, sin_t


def _largest_tile(n, cap, mult):
    assert n % mult == 0, (n, mult)
    best = mult
    for t in range(mult, min(n, cap) + 1, mult):
        if n % t == 0:
            best = t
    return best


def _layer(x, layer, wts, tabs, cfg, shift_row, s0, past=None):
    bv, lv, _ = x.shape
    bs, ls = cfg["bs"], cfg["ls"]
    seq = lambda a: a.reshape(bs, ls, a.shape[-1])
    za, zq, zgb, zg = _in_proj(x, wts["norm_w"], wts["w_in"], layer, cfg["tm"], cfg["n_pad"])
    za_s = seq(za)
    prep = _rwkv_prep(za_s, shift_row, wts, layer, cfg["tm_prep"], cfg["chunk"])
    ya, s_fin = _rwkv_chain(prep, s0, wts, layer, cfg["nb"], cfg["nc"], cfg["chunk"], cfg["g"])
    q, lat, krg = _mla_q(zq, tabs, wts, layer, cfg["tm"])
    k, v = _mla_expand(lat, krg, wts, layer, cfg["tm"])
    if past is None:
        yb = _attention_causal(seq(q), seq(k), seq(v), seq(zgb), cfg["bias"], cfg["tq"])
    else:
        past_lat, past_krg = past
        n_past = past_lat.shape[1] // bs
        kp, vp = _mla_expand(past_lat, past_krg, wts, layer, _largest_tile(bs * n_past, 1024, LANES),
                             src_layer=layer)
        cached = lambda a: a.reshape(bs, n_past, QK_WIDTH)
        yb = _attention_cached(seq(q), cached(kp), cached(vp), seq(k), seq(v), seq(zgb))
    x_new = _out_proj(x, ya.reshape(bv, lv, A_WIDTH), yb.reshape(bv, lv, B_WIDTH), zg, wts, layer, cfg["tm"])
    return x_new, s_fin, za_s[:, -1:, :], seq(lat), seq(krg)[:, :, ROPE_LO:ROPE_LO + QK_ROPE]


def kernel(x_prompt, x_sample, state_rwkv, state_shift, cache_mla_latent, cache_mla_krope, meta_tokens, norm_w, w_in, shift_mix, rwkv_w0, rwkv_w2, rwkv_a0, rwkv_a2, rwkv_k_k, rwkv_k_a, rwkv_r_k, rwkv_ln_w, rwkv_ln_b, mla_q_norm, mla_w_uq, mla_kv_norm, mla_w_ukv, mla_qn_nope, mla_kn_nope, mla_qn_rope, mla_kn_rope, w_branch_a, w_branch_b, w_out):
    depth = w_in.shape[0]
    bp, seq_len, _ = x_prompt.shape
    bd, dec_len, _ = x_sample.shape
    past_len = cache_mla_latent.shape[2]
    assert seq_len % LANES == 0 and dec_len % 8 == 0
    wts = _prep_weights(w_in, shift_mix, rwkv_w0, rwkv_w2, rwkv_a0, rwkv_a2, rwkv_k_k, rwkv_k_a, rwkv_r_k,
                        rwkv_ln_w, rwkv_ln_b, mla_q_norm, mla_w_uq, mla_kv_norm, mla_w_ukv, mla_qn_nope,
                        mla_kn_nope, mla_qn_rope, mla_kn_rope, w_branch_a, w_branch_b, w_out)
    wts["norm_w"] = norm_w.reshape(depth, 1, D_MODEL)

    n_zero = PAD_FRONT - N_META
    lp = PAD_FRONT + seq_len
    meta = jnp.broadcast_to(meta_tokens[None].astype(F32), (bp, N_META, D_MODEL))
    xp = jnp.concatenate([jnp.zeros((bp, n_zero, D_MODEL), F32), meta, x_prompt], axis=1)
    tabs_p = _rope_tables(jnp.arange(lp, dtype=jnp.int32) - PAD_FRONT)
    tile_p = _largest_tile(lp, 512, LANES)
    cfg_p = dict(bs=bp, ls=lp, tm=tile_p, n_pad=n_zero, tm_prep=_largest_tile(lp, 256, CHUNK),
                 chunk=CHUNK, g=4, nb=4, nc=1, tq=tile_p, bias=_prompt_bias(tile_p, n_zero))
    zero_row = jnp.zeros((bp, 1, A_COLS), F32)
    zero_state = jnp.zeros((bp, A_HEADS, A_HEAD_DIM, A_HEAD_DIM), F32)

    n_rows = bd * dec_len
    xs = x_sample.reshape(1, n_rows, D_MODEL)
    t_s = past_len + (jnp.arange(n_rows, dtype=jnp.int32) % dec_len)
    tabs_s = _rope_tables(t_s)
    cfg_s = dict(bs=bd, ls=dec_len, tm=_largest_tile(n_rows, 512, 8), n_pad=0, tm_prep=dec_len,
                 chunk=dec_len, g=A_HEADS, nb=4, nc=1, tq=dec_len)

    past_lat = cache_mla_latent.reshape(depth, bd * past_len, KV_LORA)
    past_krg = jnp.pad(cache_mla_krope.reshape(depth, bd * past_len, QK_ROPE),
                       ((0, 0), (0, 0), (ROPE_LO, LANES - ROPE_LO - QK_ROPE)))

    outs_p, outs_s = [], []
    for l in range(depth):
        xp, *op = _layer(xp, l, wts, tabs_p, cfg_p, zero_row, zero_state)
        xs, *os_ = _layer(xs, l, wts, tabs_s, cfg_s, state_shift[l], state_rwkv[l], past=(past_lat, past_krg))
        outs_p.append(op)
        outs_s.append(os_)
    first = PAD_FRONT - N_META
    stack = lambda outs, i, lo=0: jnp.stack([o[i][:, lo:] for o in outs])
    return (xp[:, PAD_FRONT:], xs.reshape(bd, dec_len, D_MODEL),
            stack(outs_p, 0), stack(outs_p, 1), stack(outs_p, 2, first), stack(outs_p, 3, first),
            stack(outs_s, 0), stack(outs_s, 1), stack(outs_s, 2), stack(outs_s, 3))
```

```python
import functools

import numpy as np
import jax
import jax.numpy as jnp
from jax import lax
from jax.experimental import pallas as pl
from jax.experimental.pallas import tpu as pltpu

F32 = jnp.float32
BF16 = jnp.bfloat16

D_MODEL = 1024
CHUNK = 64
N_META = 16
A_HEADS = 8
A_HEAD_DIM = 64
A_WIDTH = A_HEADS * A_HEAD_DIM
DECAY_LORA = 64
ICLR_LORA = 64
B_HEADS = 8
QK_NOPE = 64
QK_ROPE = 32
QK_DIM = QK_NOPE + QK_ROPE
V_DIM = 64
B_WIDTH = B_HEADS * V_DIM
Q_LORA = 384
KV_LORA = 256
ROPE_THETA = 10000.0
NORM_EPS = 1e-6
GN_EPS = 64e-5
NEG_INF = -1e30
A_COLS = 4 * A_WIDTH + DECAY_LORA + ICLR_LORA
B_COLS = Q_LORA + KV_LORA + QK_ROPE + B_WIDTH
G_COLS = 2 * D_MODEL

LANES = 128
HEAD_PAD = LANES
QK_WIDTH = B_HEADS * HEAD_PAD
ZQ_COLS = Q_LORA + KV_LORA + LANES
ROPE_LO = QK_NOPE
PAD_FRONT = 128
VMEM_LIMIT = 56 * 1024 * 1024
BASE_BLOCK = 8


def _cparams(sem):
    return pltpu.CompilerParams(dimension_semantics=sem, vmem_limit_bytes=VMEM_LIMIT)


def _mm(a, b):
    return jnp.dot(a.astype(BF16), b.astype(BF16), preferred_element_type=F32)


def _mm_nt(a, b):
    return lax.dot_general(a.astype(BF16), b.astype(BF16), (((1,), (1,)), ((), ())),
                           preferred_element_type=F32)


def _mm_tn(a, b):
    return lax.dot_general(a.astype(BF16), b.astype(BF16), (((0,), (0,)), ((), ())),
                           preferred_element_type=F32)


def _split2(x):
    hi = x.astype(BF16)
    return hi, (x - hi.astype(F32)).astype(BF16)


def _mm_x_exact(x, e):
    hi, lo = _split2(x)
    return jnp.dot(hi, e, preferred_element_type=F32) + jnp.dot(lo, e, preferred_element_type=F32)


def _mm_exact_x(e, x):
    hi, lo = _split2(x)
    return jnp.dot(e, hi, preferred_element_type=F32) + jnp.dot(e, lo, preferred_element_type=F32)


def _sigmoid(x):
    return 1.0 / (1.0 + jnp.exp(-x))


def _in_proj_kernel(x_ref, nw_ref, w_ref, za_ref, zq_ref, zgb_ref, zg_ref, *, tm, n_pad):
    x = x_ref[0]
    ms = jnp.mean(x * x, axis=-1, keepdims=True)
    h = x * lax.rsqrt(ms + NORM_EPS) * nw_ref[0]
    if n_pad:
        row = pl.program_id(1) * tm + lax.broadcasted_iota(jnp.int32, (tm, 1), 0)
        h = jnp.where(row >= n_pad, h, 0.0)
    hb = h.astype(BF16)
    o = 0
    for ref, width in ((za_ref, A_COLS), (zq_ref, ZQ_COLS), (zgb_ref, B_WIDTH), (zg_ref, G_COLS)):
        ref[0] = jnp.dot(hb, w_ref[0, :, o:o + width], preferred_element_type=F32)
        o += width


def _in_proj(x, norm_w, w_in_p, layer, tm, n_pad):
    bv, lv, _ = x.shape
    cols = w_in_p.shape[-1]
    row = lambda w: pl.BlockSpec((1, tm, w), lambda b, j: (b, j, 0))
    return pl.pallas_call(
        functools.partial(_in_proj_kernel, tm=tm, n_pad=n_pad),
        grid=(bv, lv // tm),
        in_specs=[row(D_MODEL),
                  pl.BlockSpec((1, 1, D_MODEL), lambda b, j: (layer, 0, 0)),
                  pl.BlockSpec((1, D_MODEL, cols), lambda b, j: (layer, 0, 0))],
        out_specs=[row(A_COLS), row(ZQ_COLS), row(B_WIDTH), row(G_COLS)],
        out_shape=[jax.ShapeDtypeStruct((bv, lv, w), F32) for w in (A_COLS, ZQ_COLS, B_WIDTH, G_COLS)],
        compiler_params=_cparams(("parallel", "parallel")),
        name="in_proj",
    )(x, norm_w, w_in_p)


def _rwkv_prep_kernel(za_ref, zap_ref, sh_ref, mix_ref, w0_ref, a0_ref, lora_ref, kk_ref, ka_ref,
                      rk_ref, e_ref, tri_ref,
                      at_ref, rt_ref, bt_ref, kt_ref, vb_ref, pc_ref, bonus_ref, sg_ref, *, tm):
    j = pl.program_id(1)
    a = za_ref[0]
    first = jnp.where(j == 0, sh_ref[0], zap_ref[0, 7:8, :])
    rowid = lax.broadcasted_iota(jnp.int32, (tm, 1), 0)
    prev = jnp.where(rowid == 0, first, pltpu.roll(a, 1, 0))
    xs = a * mix_ref[0, 0:1, :] + prev * mix_ref[0, 1:2, :]
    r = xs[:, 0:A_WIDTH]
    k = xs[:, A_WIDTH:2 * A_WIDTH]
    v = xs[:, 2 * A_WIDTH:3 * A_WIDTH]
    gate = xs[:, 3 * A_WIDTH:4 * A_WIDTH]
    lr = xs[:, 4 * A_WIDTH:A_COLS]
    lane = lax.broadcasted_iota(jnp.int32, (1, DECAY_LORA + ICLR_LORA), 1)
    lr = jnp.where(lane < DECAY_LORA, jnp.tanh(lr), lr)
    lo = _mm(lr, lora_ref[0])
    wlin = w0_ref[0] + lo[:, 0:A_WIDTH]
    alin = a0_ref[0] + lo[:, A_WIDTH:2 * A_WIDTH]
    softplus_neg = jnp.maximum(-wlin, 0.0) + jnp.log1p(jnp.exp(-jnp.abs(wlin)))
    logdec = -jnp.exp(-softplus_neg - 0.5)
    av = _sigmoid(alin)
    e = e_ref[...]
    kk = k * kk_ref[0]
    kk = kk * lax.rsqrt(jnp.maximum(_mm_x_exact(kk * kk, e), 1e-24))
    kmod = k * (1.0 + (av - 1.0) * ka_ref[0])
    sums = _mm_exact_x(tri_ref[...], logdec)
    cum, tot = sums[:tm], sums[tm:]
    pin = jnp.exp(cum)
    pex = jnp.exp(cum - logdec)
    pinv = jnp.exp(-cum)
    at_ref[0] = (-kk * pex).astype(BF16)
    rt_ref[0] = (r * pin).astype(BF16)
    bt_ref[0] = (kk * av * pinv).astype(BF16)
    kt_ref[0] = (kmod * pinv).astype(BF16)
    vb_ref[0] = v.astype(BF16)
    pc_ref[0] = jnp.exp(tot)
    bonus_ref[0] = _mm_x_exact(r * kmod * rk_ref[0], e) * v
    sg_ref[0] = gate * _sigmoid(gate)


def _rwkv_prep(za, shift_row, wts, layer, tm, chunk):
    bs, ls, _ = za.shape
    ids = np.arange(tm)
    same = (ids[:, None] // chunk) == (ids[None, :] // chunk)
    tri = same & (ids[None, :] <= ids[:, None])
    e = (np.arange(A_WIDTH)[:, None] // A_HEAD_DIM) == (np.arange(A_WIDTH)[None, :] // A_HEAD_DIM)
    row = lambda w: pl.BlockSpec((1, tm, w), lambda b, j: (b, j, 0))
    par = lambda r, c: pl.BlockSpec((1, r, c), lambda b, j: (layer, 0, 0))
    const = lambda r, c: pl.BlockSpec((r, c), lambda b, j: (0, 0))
    nsub = tm // 8
    outs = [jax.ShapeDtypeStruct((bs, ls, A_WIDTH), BF16)] * 5 + [jax.ShapeDtypeStruct((bs, ls, A_WIDTH), F32)] * 3
    return pl.pallas_call(
        functools.partial(_rwkv_prep_kernel, tm=tm),
        grid=(bs, ls // tm),
        in_specs=[row(A_COLS),
                  pl.BlockSpec((1, 8, A_COLS), lambda b, j: (b, jnp.maximum(j * nsub - 1, 0), 0)),
                  pl.BlockSpec((1, 1, A_COLS), lambda b, j: (b, 0, 0)),
                  par(2, A_COLS), par(1, A_WIDTH), par(1, A_WIDTH),
                  par(DECAY_LORA + ICLR_LORA, 2 * A_WIDTH),
                  par(1, A_WIDTH), par(1, A_WIDTH), par(1, A_WIDTH),
                  const(A_WIDTH, A_WIDTH), const(2 * tm, tm)],
        out_specs=[row(A_WIDTH)] * 8,
        out_shape=outs,
        compiler_params=_cparams(("parallel", "parallel")),
        name="rwkv_prep",
    )(za, za, shift_row, wts["shift_mix"], wts["w0"], wts["a0"], wts["lora"], wts["k_k"], wts["k_a"],
      wts["r_k"], jnp.asarray(e, BF16), jnp.asarray(np.concatenate([tri, same], axis=0), BF16))


def _chain_masks(chunk, g):
    gc, w = g * chunk, g * A_HEAD_DIM
    t = np.arange(chunk)[:, None]
    s = np.arange(gc)[None, :] % chunk
    wide = {"strict": s < t, "incl": s <= t, "eye": s == t,
            "base": (s // BASE_BLOCK == t // BASE_BLOCK) & (s < t)}
    levels = []
    sz = BASE_BLOCK
    while sz < chunk:
        levels.append((s // (2 * sz) == t // (2 * sz)) & (s // sz != t // sz) & (s < t))
        sz *= 2
    rows = np.arange(gc)[:, None] // chunk
    bd_nat = rows == (np.arange(w)[None, :] // A_HEAD_DIM)
    bd_wide = rows == (np.arange(gc)[None, :] // chunk)
    bd_state = (np.arange(w)[:, None] // A_HEAD_DIM) == (np.arange(w)[None, :] // A_HEAD_DIM)
    f = lambda m: jnp.asarray(m, F32)
    b = lambda m: jnp.asarray(m, BF16)
    lev = np.stack(levels) if levels else np.zeros((1, chunk, gc), bool)
    return dict(strict=f(wide["strict"]), incl=f(wide["incl"]), eye=f(wide["eye"]), base=f(wide["base"]),
                lev=f(lev), bd_nat=b(bd_nat), bd_wide=b(bd_wide), bd_state=f(bd_state)), len(levels)


def _rwkv_chain_kernel(at_ref, rt_ref, bt_ref, kt_ref, vb_ref, pc_ref, bonus_ref, sg_ref, s0_ref,
                       lnw_ref, lnb_ref, e_ref, strict_ref, incl_ref, eye_ref, base_ref, lev_ref,
                       bdn_ref, bdw_ref, bds_ref, rep_ref, rept_ref, ya_ref, sout_ref, s_ref, *,
                       nb, nc, chunk, g, n_lev):
    c = chunk
    gc, w = g * c, g * A_HEAD_DIM
    n_groups = A_HEADS // g

    @pl.when(pl.program_id(1) == 0)
    def _():
        for b in range(nb):
            for grp in range(n_groups):
                stacked = s0_ref[b, grp * g:(grp + 1) * g].reshape(w, A_HEAD_DIM)
                s_ref[b, grp] = _mm_x_exact(stacked, rep_ref[...]) * bds_ref[...]

    strict, incl, eye, base = strict_ref[...], incl_ref[...], eye_ref[...], base_ref[...]
    bdn, bdw, bds = bdn_ref[...], bdw_ref[...], bds_ref[...]

    def bd(x, mask):
        return jnp.concatenate([x.astype(BF16)] * g, axis=0) * mask

    def wide_mm(x, y):
        return jnp.dot(x.astype(BF16), bd(y, bdw), preferred_element_type=F32)

    chains = [(b, grp) for b in range(nb) for grp in range(n_groups)]
    lanes = [slice(grp * w, (grp + 1) * w) for _, grp in chains]
    each = lambda fn, *lists: [fn(*args) for args in zip(*lists)]
    dot = lambda x, y: jnp.dot(x.astype(BF16), y, preferred_element_type=F32)
    cat0 = lambda x, y: jnp.concatenate([x, y], axis=0)
    states = [s_ref[b, grp] for b, grp in chains]
    y_chunks = [[] for _ in chains]
    for ci in range(nc):
        rows = slice(ci * c, (ci + 1) * c)
        load = lambda ref: [ref[b, rows, sl] for (b, _), sl in zip(chains, lanes)]
        at, rt, bt, kt, vb = (load(ref) for ref in (at_ref, rt_ref, bt_ref, kt_ref, vb_ref))
        ar = each(cat0, at, rt)
        gm = each(lambda a, b_, k_: _mm_nt(a, cat0(bd(b_, bdn), bd(k_, bdn))), ar, bt, kt)
        m_ab = [g_[:c, :gc] * strict for g_ in gm]
        d1 = [m * base for m in m_ab]
        d2 = each(wide_mm, d1, d1)
        t = [eye + d for d in d1]
        t = each(lambda t_, d: t_ + wide_mm(t_, d), t, d2)
        d4 = each(wide_mm, d2, d2)
        t = each(lambda t_, d: t_ + wide_mm(t_, d), t, d4)
        for lv in range(n_lev):
            off = each(lambda m, t_: wide_mm(m * lev_ref[lv], t_), m_ab, t)
            t = each(lambda t_, o: t_ + wide_mm(t_, o), t, off)
        vbd = [bd(v, bdn) for v in vb]
        akv = each(lambda g_, v: dot(g_[:c, gc:] * strict, v), gm, vbd)
        ah = each(_mm_nt, ar, states)
        u = each(lambda t_, a, k_: dot(t_, bd(a[:c] + k_, bdn)).astype(BF16), t, ah, akv)
        y = each(lambda a, g_, u_, v: a[c:] + dot(g_[c:] * jnp.concatenate([incl, incl], axis=1),
                                                  cat0(bd(u_, bdn), v)), ah, gm, u, vbd)
        upd = each(lambda u_, v, b_, k_: _mm_tn(cat0(u_, v), cat0(b_, k_)), u, vb, bt, kt)
        states = [(s + up * bds) * pc_ref[b, ci * c:ci * c + 1, sl]
                  for s, up, (b, _), sl in zip(states, upd, chains, lanes)]
        for yc, y_ in zip(y_chunks, y):
            yc.append(y_)
    for (b, grp), s in zip(chains, states):
        s_ref[b, grp] = s
    seqs = [yc[0] if nc == 1 else jnp.concatenate(yc, axis=0) for yc in y_chunks]
    ys = [seqs[b * n_groups] if n_groups == 1 else jnp.concatenate(seqs[b * n_groups:(b + 1) * n_groups], axis=1)
          for b in range(nb)]
    y = ys[0] if nb == 1 else jnp.concatenate(ys, axis=0)
    e = e_ref[...]
    inv_n = 1.0 / A_HEAD_DIM
    mu = _mm_x_exact(y, e) * inv_n
    d = y - mu
    var = _mm_x_exact(d * d, e) * inv_n
    yn = d * lax.rsqrt(var + GN_EPS) * lnw_ref[0] + lnb_ref[0]
    bonus = bonus_ref[...].reshape(nb * nc * c, A_WIDTH)
    sg = sg_ref[...].reshape(nb * nc * c, A_WIDTH)
    ya_ref[...] = ((yn + bonus) * sg).astype(BF16).reshape(nb, nc * c, A_WIDTH)

    @pl.when(pl.program_id(1) == pl.num_programs(1) - 1)
    def _():
        for (b, grp), s in zip(chains, states):
            blocks = _mm_x_exact(s, rept_ref[...])
            sout_ref[b, grp * g:(grp + 1) * g] = blocks.reshape(g, A_HEAD_DIM, A_HEAD_DIM)


def _rwkv_chain(prep, s0, wts, layer, nb, nc, chunk, g):
    at, rt, bt, kt, vb, pc, bonus, sg = prep
    bs, ls, _ = at.shape
    gc, w = g * chunk, g * A_HEAD_DIM
    n_groups = A_HEADS // g
    masks, n_lev = _chain_masks(chunk, g)
    e = (np.arange(A_WIDTH)[:, None] // A_HEAD_DIM) == (np.arange(A_WIDTH)[None, :] // A_HEAD_DIM)
    row = pl.BlockSpec((nb, nc * chunk, A_WIDTH), lambda b, j: (b, j, 0))
    par = pl.BlockSpec((1, 1, A_WIDTH), lambda b, j: (layer, 0, 0))
    state = pl.BlockSpec((nb, A_HEADS, A_HEAD_DIM, A_HEAD_DIM), lambda b, j: (b, 0, 0, 0))
    const = lambda shape: pl.BlockSpec(shape, lambda b, j: (0,) * len(shape))
    rep = np.tile(np.eye(A_HEAD_DIM), (1, g))
    return pl.pallas_call(
        functools.partial(_rwkv_chain_kernel, nb=nb, nc=nc, chunk=chunk, g=g, n_lev=n_lev),
        grid=(bs // nb, ls // (nc * chunk)),
        in_specs=[row] * 8 + [state, par, par, const((A_WIDTH, A_WIDTH)),
                              const((chunk, gc)), const((chunk, gc)), const((chunk, gc)), const((chunk, gc)),
                              const(tuple(masks["lev"].shape)),
                              const((gc, w)), const((gc, gc)), const((w, w)),
                              const((A_HEAD_DIM, w)), const((w, A_HEAD_DIM))],
        out_specs=[row, state],
        out_shape=[jax.ShapeDtypeStruct((bs, ls, A_WIDTH), BF16),
                   jax.ShapeDtypeStruct((bs, A_HEADS, A_HEAD_DIM, A_HEAD_DIM), F32)],
        scratch_shapes=[pltpu.VMEM((nb, n_groups, w, w), F32)],
        compiler_params=_cparams(("parallel", "arbitrary")),
        name="rwkv_chain",
    )(at, rt, bt, kt, vb, pc, bonus, sg, s0, wts["ln_w"], wts["ln_b"], jnp.asarray(e, BF16),
      masks["strict"], masks["incl"], masks["eye"], masks["base"], masks["lev"],
      masks["bd_nat"], masks["bd_wide"], masks["bd_state"], jnp.asarray(rep, BF16), jnp.asarray(rep.T, BF16))


def _segment_tables(bounds):
    sel = np.zeros((QK_WIDTH, LANES), np.float32)
    inv_len = np.ones((1, LANES), np.float32)
    for col, (lo, hi) in enumerate(bounds):
        sel[lo:hi, col] = 1.0
        inv_len[0, col] = 1.0 / (hi - lo)
    return jnp.asarray(sel, BF16), jnp.asarray(sel.T, BF16), jnp.asarray(inv_len, F32)


def _segment_rms_scale(x, sel_ref, selt_ref, ilen_ref):
    ss = _mm(x * x, sel_ref[...])
    return _mm_x_exact(lax.rsqrt(ss * ilen_ref[...] + NORM_EPS), selt_ref[...])


def _mla_q_kernel(zq_ref, cos_ref, sin_ref, qn_ref, wuq_ref, wrot_ref, gq_ref, gqr_ref, kvn_ref, gk_ref,
                  sel_ref, selt_ref, ilen_ref, q_ref, lat_ref, krg_ref):
    zq = zq_ref[0]
    qc = zq[:, 0:Q_LORA]
    ckv = zq[:, Q_LORA:Q_LORA + KV_LORA]
    kr = zq[:, Q_LORA + KV_LORA:ZQ_COLS]
    rms = lambda x, n: x * lax.rsqrt(jnp.sum(x * x, axis=-1, keepdims=True) * (1.0 / n) + NORM_EPS)
    qn = (rms(qc, Q_LORA) * qn_ref[0]).astype(BF16)
    qf = jnp.dot(qn, wuq_ref[0], preferred_element_type=F32)
    qp = jnp.dot(qn, wrot_ref[0], preferred_element_type=F32)
    cos, sin = cos_ref[...], sin_ref[...]
    cos_all = jnp.concatenate([cos] * B_HEADS, axis=1)
    sin_all = jnp.concatenate([sin] * B_HEADS, axis=1)
    scale = _segment_rms_scale(qf, sel_ref, selt_ref, ilen_ref)
    q_ref[0] = (scale * (qf * gq_ref[0] * cos_all + qp * gqr_ref[0] * sin_all)).astype(BF16)
    lat_ref[0] = rms(ckv, KV_LORA) * kvn_ref[0]
    krn = rms(kr, QK_ROPE) * gk_ref[0]
    half = QK_ROPE // 2
    lane = lax.broadcasted_iota(jnp.int32, (1, LANES), 1)
    partner = jnp.where(lane < ROPE_LO + half, pltpu.roll(krn, LANES - half, 1), pltpu.roll(krn, half, 1))
    krg_ref[0] = krn * cos + partner * sin


def _mla_q(zq, tabs, wts, layer, tm):
    bv, lv, _ = zq.shape
    row = lambda w: pl.BlockSpec((1, tm, w), lambda b, j: (b, j, 0))
    tab = pl.BlockSpec((tm, LANES), lambda b, j: (j, 0))
    par = lambda r, c: pl.BlockSpec((1, r, c), lambda b, j: (layer, 0, 0))
    const = lambda a: pl.BlockSpec(a.shape, lambda b, j: (0, 0))
    segs = _segment_tables([b for h in range(B_HEADS) for b in
                            ((h * HEAD_PAD, h * HEAD_PAD + QK_NOPE), (h * HEAD_PAD + QK_NOPE, h * HEAD_PAD + QK_DIM))])
    return pl.pallas_call(
        _mla_q_kernel,
        grid=(bv, lv // tm),
        in_specs=[row(ZQ_COLS), tab, tab, par(1, Q_LORA), par(Q_LORA, QK_WIDTH), par(Q_LORA, QK_WIDTH),
                  par(1, QK_WIDTH), par(1, QK_WIDTH), par(1, KV_LORA), par(1, LANES)] + [const(a) for a in segs],
        out_specs=[row(QK_WIDTH), row(KV_LORA), row(LANES)],
        out_shape=[jax.ShapeDtypeStruct((bv, lv, QK_WIDTH), BF16),
                   jax.ShapeDtypeStruct((bv, lv, KV_LORA), F32),
                   jax.ShapeDtypeStruct((bv, lv, LANES), F32)],
        compiler_params=_cparams(("parallel", "parallel")),
        name="mla_q",
    )(zq, tabs[0], tabs[1], wts["q_norm"], wts["w_uq"], wts["w_uq_rot"], wts["g_q"], wts["g_q_rot"],
      wts["kv_norm"], wts["g_kr"], *segs)


def _expand_rows(lat_ref, krg_ref, ex_refs, k_ref, v_ref, src_rows, dst_rows):
    wk_ref, wv_ref, gk_ref, sel_ref, selt_ref, ilen_ref = ex_refs
    lat = lat_ref[0, src_rows, :].astype(BF16)
    kf = jnp.dot(lat, wk_ref[0], preferred_element_type=F32)
    scale = _segment_rms_scale(kf, sel_ref, selt_ref, ilen_ref)
    krg = jnp.concatenate([krg_ref[0, src_rows, :]] * B_HEADS, axis=1)
    k_ref[0, dst_rows, :] = (kf * scale * gk_ref[0] + krg).astype(BF16)
    lane = lax.broadcasted_iota(jnp.int32, (1, QK_WIDTH), 1)
    ones_col = jnp.where(lane % HEAD_PAD == V_DIM, 1.0, 0.0)
    v_ref[0, dst_rows, :] = (jnp.dot(lat, wv_ref[0], preferred_element_type=F32) + ones_col).astype(BF16)


def _expand_operands(wts, layer):
    segs = _segment_tables([(h * HEAD_PAD, h * HEAD_PAD + QK_NOPE) for h in range(B_HEADS)])
    par = lambda r, c: pl.BlockSpec((1, r, c), lambda *g: (layer, 0, 0))
    specs = [par(KV_LORA, QK_WIDTH), par(KV_LORA, QK_WIDTH), par(1, QK_WIDTH)]
    specs += [pl.BlockSpec(a.shape, lambda *g: (0, 0)) for a in segs]
    return specs, (wts["w_uk"], wts["w_uv"], wts["g_kn"], *segs)


def _attn_block(q_ref, k_ref, v_ref, start, tk, bias, m_ref, acc_ref):
    rows = pl.ds(start, tk)

    def scores(h):
        lanes = slice(h * HEAD_PAD, (h + 1) * HEAD_PAD)
        s = _mm_nt(q_ref[0, :, lanes], k_ref[0, rows, lanes])
        return s if bias is None else s + bias

    s_next = scores(0)
    for h in range(B_HEADS):
        s = s_next
        if h + 1 < B_HEADS:
            s_next = scores(h + 1)
        m_old = m_ref[h]
        m_new = jnp.maximum(m_old, jnp.max(s, axis=-1, keepdims=True))
        m_wide = m_new[:, :tk] if tk < LANES else jnp.concatenate([m_new] * (tk // LANES), axis=1)
        p = jnp.exp2(s - m_wide)
        pv = jnp.dot(p.astype(BF16), v_ref[0, rows, h * HEAD_PAD:(h + 1) * HEAD_PAD],
                     preferred_element_type=F32)
        acc_ref[h] = jnp.exp2(m_old - m_new) * acc_ref[h] + pv
        m_ref[h] = m_new


def _attn_finish(gb_ref, o_ref, acc_ref):
    outs = []
    for h in range(B_HEADS):
        acc = acc_ref[h]
        outs.append(acc[:, 0:V_DIM] / acc[:, V_DIM:V_DIM + 1])
    gb = gb_ref[0]
    o_ref[0] = (jnp.concatenate(outs, axis=1) * (gb * _sigmoid(gb))).astype(BF16)


def _attn_causal_kernel(q_ref, lat_ref, krg_ref, gb_ref, bias_ref, *rest, t):
    *ex_refs, o_ref, m_ref, acc_ref, k_ref, v_ref = rest
    qi = pl.program_id(1)

    @pl.when(qi == 0)
    def _():
        def expand(r, carry):
            rows = pl.ds(pl.multiple_of(r * t, t), t)
            _expand_rows(lat_ref, krg_ref, ex_refs, k_ref, v_ref, rows, rows)
            return carry

        lax.fori_loop(0, lat_ref.shape[1] // t, expand, 0)

    m_ref[...] = jnp.full(m_ref.shape, NEG_INF, F32)
    acc_ref[...] = jnp.zeros(acc_ref.shape, F32)
    block = functools.partial(_attn_block, q_ref, k_ref, v_ref, tk=t, m_ref=m_ref, acc_ref=acc_ref)
    block(start=0, bias=bias_ref[jnp.where(qi == 0, 2, 1)])

    def kv_step(c, carry):
        block(start=pl.multiple_of(c * t, t), bias=None)
        return carry

    lax.fori_loop(1, qi, kv_step, 0)

    @pl.when(qi > 0)
    def _():
        block(start=pl.multiple_of(qi * t, t), bias=bias_ref[0])

    _attn_finish(gb_ref, o_ref, acc_ref)


def _attention_causal(q, lat, krg, gb, bias, wts, layer, t):
    bs, lq, _ = q.shape
    whole = lambda w: pl.BlockSpec((1, lq, w), lambda b, i: (b, 0, 0))
    ex_specs, ex_args = _expand_operands(wts, layer)
    return pl.pallas_call(
        functools.partial(_attn_causal_kernel, t=t),
        grid=(bs, lq // t),
        in_specs=[pl.BlockSpec((1, t, QK_WIDTH), lambda b, i: (b, i, 0)),
                  whole(KV_LORA), whole(LANES),
                  pl.BlockSpec((1, t, B_WIDTH), lambda b, i: (b, i, 0)),
                  pl.BlockSpec(bias.shape, lambda b, i: (0, 0, 0), pipeline_mode=pl.Buffered(1))] + ex_specs,
        out_specs=pl.BlockSpec((1, t, B_WIDTH), lambda b, i: (b, i, 0)),
        out_shape=jax.ShapeDtypeStruct((bs, lq, B_WIDTH), BF16),
        scratch_shapes=[pltpu.VMEM((B_HEADS, t, LANES), F32), pltpu.VMEM((B_HEADS, t, HEAD_PAD), F32),
                        pltpu.VMEM((1, lq, QK_WIDTH), BF16), pltpu.VMEM((1, lq, QK_WIDTH), BF16)],
        compiler_params=_cparams(("parallel", "arbitrary")),
        name="attention",
    )(q, lat, krg, gb, bias, *ex_args)


def _prompt_bias(t, n_invisible):
    i = np.arange(t)
    diag = np.where((i[None, :] // CHUNK) <= (i[:, None] // CHUNK), 0.0, NEG_INF)
    first = np.broadcast_to(np.where(i[None, :] >= n_invisible, 0.0, NEG_INF), (t, t))
    return jnp.asarray(np.stack([diag, first, np.minimum(diag, first)]), F32)


def _attn_cached_kernel(q_ref, latp_ref, krgp_ref, latn_ref, krgn_ref, gb_ref, *rest, rows_per_step):
    *ex_refs, o_ref, m_ref, acc_ref, kp_ref, vp_ref, kn_ref, vn_ref = rest
    n_past, n_new = latp_ref.shape[1], latn_ref.shape[1]
    for r in range(n_past // rows_per_step):
        rows = pl.ds(r * rows_per_step, rows_per_step)
        _expand_rows(latp_ref, krgp_ref, ex_refs, kp_ref, vp_ref, rows, rows)
    _expand_rows(latn_ref, krgn_ref, ex_refs, kn_ref, vn_ref, pl.ds(0, n_new), pl.ds(0, n_new))
    m_ref[...] = jnp.full(m_ref.shape, NEG_INF, F32)
    acc_ref[...] = jnp.zeros(acc_ref.shape, F32)
    _attn_block(q_ref, kp_ref, vp_ref, 0, n_past, None, m_ref, acc_ref)
    _attn_block(q_ref, kn_ref, vn_ref, 0, n_new, None, m_ref, acc_ref)
    _attn_finish(gb_ref, o_ref, acc_ref)


def _attention_cached(q, lat_past, krg_past, lat_new, krg_new, gb, wts, layer):
    bs, lq, _ = q.shape
    n_past = lat_past.shape[1]
    seq = lambda n, w: pl.BlockSpec((1, n, w), lambda b: (b, 0, 0))
    cached = lambda w: pl.BlockSpec((1, n_past, w), lambda b: (layer * bs + b, 0, 0))
    ex_specs, ex_args = _expand_operands(wts, layer)
    return pl.pallas_call(
        functools.partial(_attn_cached_kernel, rows_per_step=_largest_tile(n_past, 512, LANES)),
        grid=(bs,),
        in_specs=[seq(lq, QK_WIDTH), cached(KV_LORA), cached(LANES), seq(lq, KV_LORA), seq(lq, LANES),
                  seq(lq, B_WIDTH)] + ex_specs,
        out_specs=seq(lq, B_WIDTH),
        out_shape=jax.ShapeDtypeStruct((bs, lq, B_WIDTH), BF16),
        scratch_shapes=[pltpu.VMEM((B_HEADS, lq, LANES), F32), pltpu.VMEM((B_HEADS, lq, HEAD_PAD), F32),
                        pltpu.VMEM((1, n_past, QK_WIDTH), BF16), pltpu.VMEM((1, n_past, QK_WIDTH), BF16),
                        pltpu.VMEM((1, lq, QK_WIDTH), BF16), pltpu.VMEM((1, lq, QK_WIDTH), BF16)],
        compiler_params=_cparams(("parallel",)),
        name="attention_cached",
    )(q, lat_past, krg_past, lat_new, krg_new, gb, *ex_args)


def _out_proj_kernel(x_ref, ya_ref, yb_ref, zg_ref, wa_ref, wb_ref, wo_ref, o_ref):
    zg = zg_ref[0]
    ga = _sigmoid(zg[:, 0:D_MODEL])
    gb = _sigmoid(zg[:, D_MODEL:G_COLS])
    t = (ga * jnp.dot(ya_ref[0], wa_ref[0], preferred_element_type=F32)
         + gb * jnp.dot(yb_ref[0], wb_ref[0], preferred_element_type=F32))
    o_ref[0] = x_ref[0] + _mm(t, wo_ref[0])


def _out_proj(x, ya, yb, zg, wts, layer, tm):
    bv, lv, _ = x.shape
    row = lambda w: pl.BlockSpec((1, tm, w), lambda b, j: (b, j, 0))
    par = lambda r, c: pl.BlockSpec((1, r, c), lambda b, j: (layer, 0, 0))
    return pl.pallas_call(
        _out_proj_kernel,
        grid=(bv, lv // tm),
        in_specs=[row(D_MODEL), row(A_WIDTH), row(B_WIDTH), row(G_COLS),
                  par(A_WIDTH, D_MODEL), par(B_WIDTH, D_MODEL), par(D_MODEL, D_MODEL)],
        out_specs=row(D_MODEL),
        out_shape=jax.ShapeDtypeStruct((bv, lv, D_MODEL), F32),
        compiler_params=_cparams(("parallel", "parallel")),
        name="out_proj",
    )(x, ya, yb, zg, wts["w_a"], wts["w_b"], wts["w_o"])


def _prep_weights(w_in, shift_mix, rwkv_w0, rwkv_w2, rwkv_a0, rwkv_a2, rwkv_k_k, rwkv_k_a, rwkv_r_k,
                  rwkv_ln_w, rwkv_ln_b, mla_q_norm, mla_w_uq, mla_kv_norm, mla_w_ukv, mla_qn_nope,
                  mla_kn_nope, mla_qn_rope, mla_kn_rope, w_branch_a, w_branch_b, w_out):
    depth = w_in.shape[0]
    a_end, b0 = A_COLS, A_COLS
    qc = w_in[:, :, b0:b0 + Q_LORA + KV_LORA]
    kr = w_in[:, :, b0 + Q_LORA + KV_LORA:b0 + Q_LORA + KV_LORA + QK_ROPE]
    gate_b = w_in[:, :, b0 + Q_LORA + KV_LORA + QK_ROPE:b0 + B_COLS]
    kr_grp = jnp.pad(kr, ((0, 0), (0, 0), (ROPE_LO, LANES - ROPE_LO - QK_ROPE)))
    w_in_p = jnp.concatenate([w_in[:, :, :a_end], qc, kr_grp, gate_b, w_in[:, :, A_COLS + B_COLS:]],
                             axis=-1).astype(BF16)
    zeros = jnp.zeros((depth, DECAY_LORA, A_WIDTH), F32)
    lora = jnp.concatenate([jnp.concatenate([rwkv_w2, zeros], axis=2),
                            jnp.concatenate([zeros, rwkv_a2], axis=2)], axis=1).astype(BF16)
    row = lambda p: p.reshape(depth, 1, -1)
    w_uq = mla_w_uq.reshape(depth, Q_LORA, B_HEADS, QK_DIM)
    w_uq = jnp.pad(w_uq, ((0, 0), (0, 0), (0, 0), (0, HEAD_PAD - QK_DIM)))
    g_q = jnp.pad(jnp.concatenate([mla_qn_nope, mla_qn_rope], axis=1), ((0, 0), (0, HEAD_PAD - QK_DIM)))
    half = QK_ROPE // 2

    def partner(a):
        lo, hi = a[..., ROPE_LO:ROPE_LO + half], a[..., ROPE_LO + half:ROPE_LO + QK_ROPE]
        zero = jnp.zeros_like(a[..., :ROPE_LO])
        return jnp.concatenate([zero, hi, lo, zero[..., :HEAD_PAD - ROPE_LO - QK_ROPE]], axis=-1)

    q_scale = QK_DIM ** -0.5 * float(np.log2(np.e))
    heads = lambda g: jnp.tile(g, (1, B_HEADS))
    w_uq_rot = partner(w_uq).reshape(depth, Q_LORA, QK_WIDTH)
    w_uq = w_uq.reshape(depth, Q_LORA, QK_WIDTH)
    g_q_rot = heads(partner(g_q)) * q_scale
    g_q = heads(g_q) * q_scale
    g_kr = jnp.pad(mla_kn_rope, ((0, 0), (ROPE_LO, LANES - ROPE_LO - QK_ROPE)))
    g_kn = heads(jnp.pad(mla_kn_nope, ((0, 0), (0, HEAD_PAD - QK_NOPE))))
    w_ukv = mla_w_ukv.reshape(depth, KV_LORA, B_HEADS, QK_NOPE + V_DIM)
    w_uk = jnp.pad(w_ukv[..., :QK_NOPE], ((0, 0), (0, 0), (0, 0), (0, HEAD_PAD - QK_NOPE)))
    w_uv = jnp.pad(w_ukv[..., QK_NOPE:], ((0, 0), (0, 0), (0, 0), (0, HEAD_PAD - V_DIM)))
    return dict(
        w_in=w_in_p, shift_mix=shift_mix, w0=row(rwkv_w0), a0=row(rwkv_a0), lora=lora,
        k_k=row(rwkv_k_k), k_a=row(rwkv_k_a), r_k=row(rwkv_r_k), ln_w=row(rwkv_ln_w), ln_b=row(rwkv_ln_b),
        q_norm=row(mla_q_norm), w_uq=w_uq.astype(BF16), w_uq_rot=w_uq_rot.astype(BF16), g_q=row(g_q),
        g_q_rot=row(g_q_rot), kv_norm=row(mla_kv_norm),
        g_kr=row(g_kr), g_kn=row(g_kn),
        w_uk=w_uk.reshape(depth, KV_LORA, QK_WIDTH).astype(BF16),
        w_uv=w_uv.reshape(depth, KV_LORA, QK_WIDTH).astype(BF16),
        w_a=w_branch_a.astype(BF16), w_b=w_branch_b.astype(BF16), w_o=w_out.astype(BF16))


def _rope_tables(t):
    half = QK_ROPE // 2
    inv = ROPE_THETA ** (-jnp.arange(0, QK_ROPE, 2, dtype=F32) / QK_ROPE)
    ang = t.astype(F32)[:, None] * inv[None, :]
    cos, sin = jnp.cos(ang), jnp.sin(ang)
    n = t.shape[0]
    ones = lambda w: jnp.ones((n, w), F32)
    zeros = lambda w: jnp.zeros((n, w), F32)
    tail = LANES - ROPE_LO - QK_ROPE
    cos_t = jnp.concatenate([ones(ROPE_LO), cos, cos, ones(tail)], axis=1)
    sin_t = jnp.concatenate([zeros(ROPE_LO), -sin, sin, zeros(tail)], axis=1)
    return cos_t, sin_t


def _largest_tile(n, cap, mult):
    assert n % mult == 0, (n, mult)
    best = mult
    for t in range(mult, min(n, cap) + 1, mult):
        if n % t == 0:
            best = t
    return best


def _layer(x, layer, wts, tabs, cfg, shift_row, s0, past=None):
    bv, lv, _ = x.shape
    bs, ls = cfg["bs"], cfg["ls"]
    seq = lambda a: a.reshape(bs, ls, a.shape[-1])
    za, zq, zgb, zg = _in_proj(x, wts["norm_w"], wts["w_in"], layer, cfg["tm"], cfg["n_pad"])
    za_s = seq(za)
    prep = _rwkv_prep(za_s, shift_row, wts, layer, cfg["tm_prep"], cfg["chunk"])
    ya, s_fin = _rwkv_chain(prep, s0, wts, layer, cfg["nb"], cfg["nc"], cfg["chunk"], cfg["g"])
    q, lat, krg = _mla_q(zq, tabs, wts, layer, cfg["tm"])
    if past is None:
        yb = _attention_causal(seq(q), seq(lat), seq(krg), seq(zgb), cfg["bias"], wts, layer, cfg["tq"])
    else:
        yb = _attention_cached(seq(q), past[0], past[1], seq(lat), seq(krg), seq(zgb), wts, layer)
    x_new = _out_proj(x, ya.reshape(bv, lv, A_WIDTH), yb.reshape(bv, lv, B_WIDTH), zg, wts, layer, cfg["tm"])
    return x_new, s_fin, za_s[:, -1:, :], seq(lat), seq(krg)[:, :, ROPE_LO:ROPE_LO + QK_ROPE]


def kernel(x_prompt, x_sample, state_rwkv, state_shift, cache_mla_latent, cache_mla_krope, meta_tokens, norm_w, w_in, shift_mix, rwkv_w0, rwkv_w2, rwkv_a0, rwkv_a2, rwkv_k_k, rwkv_k_a, rwkv_r_k, rwkv_ln_w, rwkv_ln_b, mla_q_norm, mla_w_uq, mla_kv_norm, mla_w_ukv, mla_qn_nope, mla_kn_nope, mla_qn_rope, mla_kn_rope, w_branch_a, w_branch_b, w_out):
    depth = w_in.shape[0]
    bp, seq_len, _ = x_prompt.shape
    bd, dec_len, _ = x_sample.shape
    past_len = cache_mla_latent.shape[2]
    assert seq_len % LANES == 0 and dec_len % 8 == 0
    wts = _prep_weights(w_in, shift_mix, rwkv_w0, rwkv_w2, rwkv_a0, rwkv_a2, rwkv_k_k, rwkv_k_a, rwkv_r_k,
                        rwkv_ln_w, rwkv_ln_b, mla_q_norm, mla_w_uq, mla_kv_norm, mla_w_ukv, mla_qn_nope,
                        mla_kn_nope, mla_qn_rope, mla_kn_rope, w_branch_a, w_branch_b, w_out)
    wts["norm_w"] = norm_w.reshape(depth, 1, D_MODEL)

    n_zero = PAD_FRONT - N_META
    lp = PAD_FRONT + seq_len
    meta = jnp.broadcast_to(meta_tokens[None].astype(F32), (bp, N_META, D_MODEL))
    xp = jnp.concatenate([jnp.zeros((bp, n_zero, D_MODEL), F32), meta, x_prompt], axis=1)
    tabs_p = _rope_tables(jnp.arange(lp, dtype=jnp.int32) - PAD_FRONT)
    tile_p = _largest_tile(lp, 512, LANES)
    cfg_p = dict(bs=bp, ls=lp, tm=tile_p, n_pad=n_zero, tm_prep=_largest_tile(lp, 256, CHUNK),
                 chunk=CHUNK, g=4, nb=8, nc=1, tq=tile_p, bias=_prompt_bias(tile_p, n_zero))
    zero_row = jnp.zeros((bp, 1, A_COLS), F32)
    zero_state = jnp.zeros((bp, A_HEADS, A_HEAD_DIM, A_HEAD_DIM), F32)

    n_rows = bd * dec_len
    xs = x_sample.reshape(1, n_rows, D_MODEL)
    t_s = past_len + (jnp.arange(n_rows, dtype=jnp.int32) % dec_len)
    tabs_s = _rope_tables(t_s)
    cfg_s = dict(bs=bd, ls=dec_len, tm=_largest_tile(n_rows, 512, 8), n_pad=0, tm_prep=dec_len,
                 chunk=dec_len, g=A_HEADS, nb=4, nc=1, tq=dec_len)

    past_lat = cache_mla_latent.reshape(depth * bd, past_len, KV_LORA)
    past_krg = jnp.pad(cache_mla_krope.reshape(depth * bd, past_len, QK_ROPE),
                       ((0, 0), (0, 0), (ROPE_LO, LANES - ROPE_LO - QK_ROPE)))

    outs_p, outs_s = [], []
    for l in range(depth):
        xp, *op = _layer(xp, l, wts, tabs_p, cfg_p, zero_row, zero_state)
        xs, *os_ = _layer(xs, l, wts, tabs_s, cfg_s, state_shift[l], state_rwkv[l], past=(past_lat, past_krg))
        outs_p.append(op)
        outs_s.append(os_)
    first = PAD_FRONT - N_META
    stack = lambda outs, i, lo=0: jnp.stack([o[i][:, lo:] for o in outs])
    return (xp[:, PAD_FRONT:], xs.reshape(bd, dec_len, D_MODEL),
            stack(outs_p, 0), stack(outs_p, 1), stack(outs_p, 2, first), stack(outs_p, 3, first),
            stack(outs_s, 0), stack(outs_s, 1), stack(outs_s, 2), stack(outs_s, 3))
```

```python
import functools

import numpy as np
import jax
import jax.numpy as jnp
from jax import lax
from jax.experimental import pallas as pl
from jax.experimental.pallas import tpu as pltpu

F32 = jnp.float32
BF16 = jnp.bfloat16

D_MODEL = 1024
CHUNK = 64
N_META = 16
A_HEADS = 8
A_HEAD_DIM = 64
A_WIDTH = A_HEADS * A_HEAD_DIM
DECAY_LORA = 64
ICLR_LORA = 64
B_HEADS = 8
QK_NOPE = 64
QK_ROPE = 32
QK_DIM = QK_NOPE + QK_ROPE
V_DIM = 64
B_WIDTH = B_HEADS * V_DIM
Q_LORA = 384
KV_LORA = 256
ROPE_THETA = 10000.0
NORM_EPS = 1e-6
GN_EPS = 64e-5
NEG_INF = -1e30
A_COLS = 4 * A_WIDTH + DECAY_LORA + ICLR_LORA
B_COLS = Q_LORA + KV_LORA + QK_ROPE + B_WIDTH
G_COLS = 2 * D_MODEL

LANES = 128
HEAD_PAD = LANES
QK_WIDTH = B_HEADS * HEAD_PAD
ZQ_COLS = Q_LORA + KV_LORA + LANES
ROPE_LO = QK_NOPE
PAD_FRONT = 128
VMEM_LIMIT = 56 * 1024 * 1024
BASE_BLOCK = 8


def _cparams(sem):
    return pltpu.CompilerParams(dimension_semantics=sem, vmem_limit_bytes=VMEM_LIMIT)


def _mm(a, b):
    return jnp.dot(a.astype(BF16), b.astype(BF16), preferred_element_type=F32)


def _mm_nt(a, b):
    return lax.dot_general(a.astype(BF16), b.astype(BF16), (((1,), (1,)), ((), ())),
                           preferred_element_type=F32)


def _mm_tn(a, b):
    return lax.dot_general(a.astype(BF16), b.astype(BF16), (((0,), (0,)), ((), ())),
                           preferred_element_type=F32)


def _split2(x):
    hi = x.astype(BF16)
    return hi, (x - hi.astype(F32)).astype(BF16)


def _mm_x_exact(x, e):
    hi, lo = _split2(x)
    return jnp.dot(hi, e, preferred_element_type=F32) + jnp.dot(lo, e, preferred_element_type=F32)


def _mm_exact_x(e, x):
    hi, lo = _split2(x)
    return jnp.dot(e, hi, preferred_element_type=F32) + jnp.dot(e, lo, preferred_element_type=F32)


def _sigmoid(x):
    return 1.0 / (1.0 + jnp.exp(-x))


def _in_proj_kernel(x_ref, nw_ref, w_ref, za_ref, zq_ref, zgb_ref, zg_ref, *, tm, n_pad):
    x = x_ref[0]
    ms = jnp.mean(x * x, axis=-1, keepdims=True)
    h = x * lax.rsqrt(ms + NORM_EPS) * nw_ref[0]
    if n_pad:
        row = pl.program_id(1) * tm + lax.broadcasted_iota(jnp.int32, (tm, 1), 0)
        h = jnp.where(row >= n_pad, h, 0.0)
    hb = h.astype(BF16)
    o = 0
    for ref, width in ((za_ref, A_COLS), (zq_ref, ZQ_COLS), (zgb_ref, B_WIDTH), (zg_ref, G_COLS)):
        ref[0] = jnp.dot(hb, w_ref[0, :, o:o + width], preferred_element_type=F32)
        o += width


def _in_proj(x, norm_w, w_in_p, layer, tm, n_pad):
    bv, lv, _ = x.shape
    cols = w_in_p.shape[-1]
    row = lambda w: pl.BlockSpec((1, tm, w), lambda b, j: (b, j, 0))
    return pl.pallas_call(
        functools.partial(_in_proj_kernel, tm=tm, n_pad=n_pad),
        grid=(bv, lv // tm),
        in_specs=[row(D_MODEL),
                  pl.BlockSpec((1, 1, D_MODEL), lambda b, j: (layer, 0, 0)),
                  pl.BlockSpec((1, D_MODEL, cols), lambda b, j: (layer, 0, 0))],
        out_specs=[row(A_COLS), row(ZQ_COLS), row(B_WIDTH), row(G_COLS)],
        out_shape=[jax.ShapeDtypeStruct((bv, lv, w), F32) for w in (A_COLS, ZQ_COLS, B_WIDTH, G_COLS)],
        compiler_params=_cparams(("parallel", "parallel")),
        name="in_proj",
    )(x, norm_w, w_in_p)


def _rwkv_prep_kernel(za_ref, zap_ref, sh_ref, mix_ref, w0_ref, a0_ref, lora_ref, kk_ref, ka_ref,
                      rk_ref, e_ref, tri_ref,
                      at_ref, rt_ref, bt_ref, kt_ref, vb_ref, pc_ref, bonus_ref, sg_ref, *, tm):
    j = pl.program_id(1)
    a = za_ref[0]
    first = jnp.where(j == 0, sh_ref[0], zap_ref[0, 7:8, :])
    rowid = lax.broadcasted_iota(jnp.int32, (tm, 1), 0)
    prev = jnp.where(rowid == 0, first, pltpu.roll(a, 1, 0))
    xs = a * mix_ref[0, 0:1, :] + prev * mix_ref[0, 1:2, :]
    r = xs[:, 0:A_WIDTH]
    k = xs[:, A_WIDTH:2 * A_WIDTH]
    v = xs[:, 2 * A_WIDTH:3 * A_WIDTH]
    gate = xs[:, 3 * A_WIDTH:4 * A_WIDTH]
    lr = xs[:, 4 * A_WIDTH:A_COLS]
    lane = lax.broadcasted_iota(jnp.int32, (1, DECAY_LORA + ICLR_LORA), 1)
    lr = jnp.where(lane < DECAY_LORA, jnp.tanh(lr), lr)
    lo = _mm(lr, lora_ref[0])
    wlin = w0_ref[0] + lo[:, 0:A_WIDTH]
    alin = a0_ref[0] + lo[:, A_WIDTH:2 * A_WIDTH]
    logdec = -float(np.exp(-0.5)) * _sigmoid(wlin)
    av = _sigmoid(alin)
    e = e_ref[...]
    kk = k * kk_ref[0]
    kk = kk * lax.rsqrt(jnp.maximum(_mm(kk * kk, e), 1e-24))
    kmod = k * (1.0 + (av - 1.0) * ka_ref[0])
    sums = _mm_exact_x(tri_ref[...], logdec)
    cum, tot = sums[:tm], sums[tm:]
    pin = jnp.exp(cum)
    pex = jnp.exp(cum - logdec)
    pinv = jnp.exp(-cum)
    at_ref[0] = (-kk * pex).astype(BF16)
    rt_ref[0] = (r * pin).astype(BF16)
    bt_ref[0] = (kk * av * pinv).astype(BF16)
    kt_ref[0] = (kmod * pinv).astype(BF16)
    vb_ref[0] = v.astype(BF16)
    pc_ref[0] = jnp.exp(tot)
    bonus_ref[0] = _mm_x_exact(r * kmod * rk_ref[0], e) * v
    sg_ref[0] = gate * _sigmoid(gate)


def _rwkv_prep(za, shift_row, wts, layer, tm, chunk):
    bs, ls, _ = za.shape
    ids = np.arange(tm)
    same = (ids[:, None] // chunk) == (ids[None, :] // chunk)
    tri = same & (ids[None, :] <= ids[:, None])
    e = (np.arange(A_WIDTH)[:, None] // A_HEAD_DIM) == (np.arange(A_WIDTH)[None, :] // A_HEAD_DIM)
    row = lambda w: pl.BlockSpec((1, tm, w), lambda b, j: (b, j, 0))
    par = lambda r, c: pl.BlockSpec((1, r, c), lambda b, j: (layer, 0, 0))
    const = lambda r, c: pl.BlockSpec((r, c), lambda b, j: (0, 0))
    nsub = tm // 8
    outs = [jax.ShapeDtypeStruct((bs, ls, A_WIDTH), BF16)] * 5 + [jax.ShapeDtypeStruct((bs, ls, A_WIDTH), F32)] * 3
    return pl.pallas_call(
        functools.partial(_rwkv_prep_kernel, tm=tm),
        grid=(bs, ls // tm),
        in_specs=[row(A_COLS),
                  pl.BlockSpec((1, 8, A_COLS), lambda b, j: (b, jnp.maximum(j * nsub - 1, 0), 0)),
                  pl.BlockSpec((1, 1, A_COLS), lambda b, j: (b, 0, 0)),
                  par(2, A_COLS), par(1, A_WIDTH), par(1, A_WIDTH),
                  par(DECAY_LORA + ICLR_LORA, 2 * A_WIDTH),
                  par(1, A_WIDTH), par(1, A_WIDTH), par(1, A_WIDTH),
                  const(A_WIDTH, A_WIDTH), const(2 * tm, tm)],
        out_specs=[row(A_WIDTH)] * 8,
        out_shape=outs,
        compiler_params=_cparams(("parallel", "parallel")),
        name="rwkv_prep",
    )(za, za, shift_row, wts["shift_mix"], wts["w0"], wts["a0"], wts["lora"], wts["k_k"], wts["k_a"],
      wts["r_k"], jnp.asarray(e, BF16), jnp.asarray(np.concatenate([tri, same], axis=0), BF16))


def _chain_masks(chunk, g):
    gc, w = g * chunk, g * A_HEAD_DIM
    t = np.arange(chunk)[:, None]
    s = np.arange(gc)[None, :] % chunk
    wide = {"strict": s < t, "incl": s <= t, "eye": s == t,
            "base": (s // BASE_BLOCK == t // BASE_BLOCK) & (s < t)}
    levels = []
    sz = BASE_BLOCK
    while sz < chunk:
        levels.append((s // (2 * sz) == t // (2 * sz)) & (s // sz != t // sz) & (s < t))
        sz *= 2
    rows = np.arange(gc)[:, None] // chunk
    bd_nat = rows == (np.arange(w)[None, :] // A_HEAD_DIM)
    bd_wide = rows == (np.arange(gc)[None, :] // chunk)
    bd_state = (np.arange(w)[:, None] // A_HEAD_DIM) == (np.arange(w)[None, :] // A_HEAD_DIM)
    f = lambda m: jnp.asarray(m, F32)
    b = lambda m: jnp.asarray(m, BF16)
    lev = np.stack(levels) if levels else np.zeros((1, chunk, gc), bool)
    return dict(strict=f(wide["strict"]), incl=f(wide["incl"]), eye=f(wide["eye"]), base=f(wide["base"]),
                lev=f(lev), bd_nat=b(bd_nat), bd_wide=b(bd_wide), bd_state=f(bd_state)), len(levels)


def _rwkv_chain_kernel(at_ref, rt_ref, bt_ref, kt_ref, vb_ref, pc_ref, bonus_ref, sg_ref, s0_ref,
                       lnw_ref, lnb_ref, e_ref, strict_ref, incl_ref, eye_ref, base_ref, lev_ref,
                       bdn_ref, bdw_ref, bds_ref, rep_ref, rept_ref, ya_ref, sout_ref, s_ref, *,
                       nb, nc, chunk, g, n_lev):
    c = chunk
    gc, w = g * c, g * A_HEAD_DIM
    n_groups = A_HEADS // g

    @pl.when(pl.program_id(1) == 0)
    def _():
        for b in range(nb):
            for grp in range(n_groups):
                stacked = s0_ref[b, grp * g:(grp + 1) * g].reshape(w, A_HEAD_DIM)
                s_ref[b, grp] = _mm_x_exact(stacked, rep_ref[...]) * bds_ref[...]

    strict, incl, eye, base = strict_ref[...], incl_ref[...], eye_ref[...], base_ref[...]
    bdn, bdw, bds = bdn_ref[...], bdw_ref[...], bds_ref[...]

    def bd(x, mask):
        return jnp.concatenate([x.astype(BF16)] * g, axis=0) * mask

    def wide_mm(x, y):
        return jnp.dot(x.astype(BF16), bd(y, bdw), preferred_element_type=F32)

    chains = [(b, grp) for b in range(nb) for grp in range(n_groups)]
    lanes = [slice(grp * w, (grp + 1) * w) for _, grp in chains]
    each = lambda fn, *lists: [fn(*args) for args in zip(*lists)]
    dot = lambda x, y: jnp.dot(x.astype(BF16), y, preferred_element_type=F32)
    cat0 = lambda x, y: jnp.concatenate([x, y], axis=0)
    states = [s_ref[b, grp] for b, grp in chains]
    y_chunks = [[] for _ in chains]
    for ci in range(nc):
        rows = slice(ci * c, (ci + 1) * c)
        load = lambda ref: [ref[b, rows, sl] for (b, _), sl in zip(chains, lanes)]
        at, rt, bt, kt, vb = (load(ref) for ref in (at_ref, rt_ref, bt_ref, kt_ref, vb_ref))
        ar = each(cat0, at, rt)
        gm = each(lambda a, b_, k_: _mm_nt(a, cat0(bd(b_, bdn), bd(k_, bdn))), ar, bt, kt)
        m_ab = [g_[:c, :gc] * strict for g_ in gm]
        d1 = [m * base for m in m_ab]
        d2 = each(wide_mm, d1, d1)
        t = [eye + d for d in d1]
        t = each(lambda t_, d: t_ + wide_mm(t_, d), t, d2)
        d4 = each(wide_mm, d2, d2)
        t = each(lambda t_, d: t_ + wide_mm(t_, d), t, d4)
        for lv in range(n_lev):
            off = each(lambda m, t_: wide_mm(m * lev_ref[lv], t_), m_ab, t)
            t = each(lambda t_, o: t_ + wide_mm(t_, o), t, off)
        vbd = [bd(v, bdn) for v in vb]
        akv = each(lambda g_, v: dot(g_[:c, gc:] * strict, v), gm, vbd)
        ah = each(_mm_nt, ar, states)
        u = each(lambda t_, a, k_: dot(t_, bd(a[:c] + k_, bdn)).astype(BF16), t, ah, akv)
        y = each(lambda a, g_, u_, v: a[c:] + dot(g_[c:] * jnp.concatenate([incl, incl], axis=1),
                                                  cat0(bd(u_, bdn), v)), ah, gm, u, vbd)
        upd = each(lambda u_, v, b_, k_: _mm_tn(cat0(u_, v), cat0(b_, k_)), u, vb, bt, kt)
        states = [(s + up * bds) * pc_ref[b, ci * c:ci * c + 1, sl]
                  for s, up, (b, _), sl in zip(states, upd, chains, lanes)]
        for yc, y_ in zip(y_chunks, y):
            yc.append(y_)
    for (b, grp), s in zip(chains, states):
        s_ref[b, grp] = s
    seqs = [yc[0] if nc == 1 else jnp.concatenate(yc, axis=0) for yc in y_chunks]
    ys = [seqs[b * n_groups] if n_groups == 1 else jnp.concatenate(seqs[b * n_groups:(b + 1) * n_groups], axis=1)
          for b in range(nb)]
    y = ys[0] if nb == 1 else jnp.concatenate(ys, axis=0)
    e = e_ref[...]
    inv_n = 1.0 / A_HEAD_DIM
    mu = _mm_x_exact(y, e) * inv_n
    d = y - mu
    var = _mm(d * d, e) * inv_n
    yn = d * lax.rsqrt(var + GN_EPS) * lnw_ref[0] + lnb_ref[0]
    bonus = bonus_ref[...].reshape(nb * nc * c, A_WIDTH)
    sg = sg_ref[...].reshape(nb * nc * c, A_WIDTH)
    ya_ref[...] = ((yn + bonus) * sg).astype(BF16).reshape(nb, nc * c, A_WIDTH)

    @pl.when(pl.program_id(1) == pl.num_programs(1) - 1)
    def _():
        for (b, grp), s in zip(chains, states):
            blocks = _mm_x_exact(s, rept_ref[...])
            sout_ref[b, grp * g:(grp + 1) * g] = blocks.reshape(g, A_HEAD_DIM, A_HEAD_DIM)


def _rwkv_chain(prep, s0, wts, layer, nb, nc, chunk, g):
    at, rt, bt, kt, vb, pc, bonus, sg = prep
    bs, ls, _ = at.shape
    gc, w = g * chunk, g * A_HEAD_DIM
    n_groups = A_HEADS // g
    masks, n_lev = _chain_masks(chunk, g)
    e = (np.arange(A_WIDTH)[:, None] // A_HEAD_DIM) == (np.arange(A_WIDTH)[None, :] // A_HEAD_DIM)
    row = pl.BlockSpec((nb, nc * chunk, A_WIDTH), lambda b, j: (b, j, 0))
    par = pl.BlockSpec((1, 1, A_WIDTH), lambda b, j: (layer, 0, 0))
    state = pl.BlockSpec((nb, A_HEADS, A_HEAD_DIM, A_HEAD_DIM), lambda b, j: (b, 0, 0, 0))
    const = lambda shape: pl.BlockSpec(shape, lambda b, j: (0,) * len(shape))
    rep = np.tile(np.eye(A_HEAD_DIM), (1, g))
    return pl.pallas_call(
        functools.partial(_rwkv_chain_kernel, nb=nb, nc=nc, chunk=chunk, g=g, n_lev=n_lev),
        grid=(bs // nb, ls // (nc * chunk)),
        in_specs=[row] * 8 + [state, par, par, const((A_WIDTH, A_WIDTH)),
                              const((chunk, gc)), const((chunk, gc)), const((chunk, gc)), const((chunk, gc)),
                              const(tuple(masks["lev"].shape)),
                              const((gc, w)), const((gc, gc)), const((w, w)),
                              const((A_HEAD_DIM, w)), const((w, A_HEAD_DIM))],
        out_specs=[row, state],
        out_shape=[jax.ShapeDtypeStruct((bs, ls, A_WIDTH), BF16),
                   jax.ShapeDtypeStruct((bs, A_HEADS, A_HEAD_DIM, A_HEAD_DIM), F32)],
        scratch_shapes=[pltpu.VMEM((nb, n_groups, w, w), F32)],
        compiler_params=_cparams(("parallel", "arbitrary")),
        name="rwkv_chain",
    )(at, rt, bt, kt, vb, pc, bonus, sg, s0, wts["ln_w"], wts["ln_b"], jnp.asarray(e, BF16),
      masks["strict"], masks["incl"], masks["eye"], masks["base"], masks["lev"],
      masks["bd_nat"], masks["bd_wide"], masks["bd_state"], jnp.asarray(rep, BF16), jnp.asarray(rep.T, BF16))


def _segment_tables(bounds):
    sel = np.zeros((QK_WIDTH, LANES), np.float32)
    inv_len = np.ones((1, LANES), np.float32)
    for col, (lo, hi) in enumerate(bounds):
        sel[lo:hi, col] = 1.0
        inv_len[0, col] = 1.0 / (hi - lo)
    return jnp.asarray(sel, BF16), jnp.asarray(sel.T, BF16), jnp.asarray(inv_len, F32)


def _segment_rms_scale(x, sel_ref, selt_ref, ilen_ref):
    ss = _mm(x * x, sel_ref[...])
    return _mm(lax.rsqrt(ss * ilen_ref[...] + NORM_EPS), selt_ref[...])


def _mla_q_kernel(zq_ref, cos_ref, sin_ref, qn_ref, wuq_ref, wrot_ref, gq_ref, gqr_ref, kvn_ref, gk_ref,
                  sel_ref, selt_ref, ilen_ref, q_ref, lat_ref, krg_ref):
    zq = zq_ref[0]
    qc = zq[:, 0:Q_LORA]
    ckv = zq[:, Q_LORA:Q_LORA + KV_LORA]
    kr = zq[:, Q_LORA + KV_LORA:ZQ_COLS]
    rms = lambda x, n: x * lax.rsqrt(jnp.sum(x * x, axis=-1, keepdims=True) * (1.0 / n) + NORM_EPS)
    qn = (rms(qc, Q_LORA) * qn_ref[0]).astype(BF16)
    qf = jnp.dot(qn, wuq_ref[0], preferred_element_type=F32)
    qp = jnp.dot(qn, wrot_ref[0], preferred_element_type=F32)
    cos, sin = cos_ref[...], sin_ref[...]
    cos_all = jnp.concatenate([cos] * B_HEADS, axis=1)
    sin_all = jnp.concatenate([sin] * B_HEADS, axis=1)
    scale = _segment_rms_scale(qf, sel_ref, selt_ref, ilen_ref)
    q_ref[0] = (scale * (qf * gq_ref[0] * cos_all + qp * gqr_ref[0] * sin_all)).astype(BF16)
    lat_ref[0] = rms(ckv, KV_LORA) * kvn_ref[0]
    krn = rms(kr, QK_ROPE) * gk_ref[0]
    half = QK_ROPE // 2
    lane = lax.broadcasted_iota(jnp.int32, (1, LANES), 1)
    partner = jnp.where(lane < ROPE_LO + half, pltpu.roll(krn, LANES - half, 1), pltpu.roll(krn, half, 1))
    krg_ref[0] = krn * cos + partner * sin


def _mla_q(zq, tabs, wts, layer, tm):
    bv, lv, _ = zq.shape
    row = lambda w: pl.BlockSpec((1, tm, w), lambda b, j: (b, j, 0))
    tab = pl.BlockSpec((tm, LANES), lambda b, j: (j, 0))
    par = lambda r, c: pl.BlockSpec((1, r, c), lambda b, j: (layer, 0, 0))
    const = lambda a: pl.BlockSpec(a.shape, lambda b, j: (0, 0))
    segs = _segment_tables([b for h in range(B_HEADS) for b in
                            ((h * HEAD_PAD, h * HEAD_PAD + QK_NOPE), (h * HEAD_PAD + QK_NOPE, h * HEAD_PAD + QK_DIM))])
    return pl.pallas_call(
        _mla_q_kernel,
        grid=(bv, lv // tm),
        in_specs=[row(ZQ_COLS), tab, tab, par(1, Q_LORA), par(Q_LORA, QK_WIDTH), par(Q_LORA, QK_WIDTH),
                  par(1, QK_WIDTH), par(1, QK_WIDTH), par(1, KV_LORA), par(1, LANES)] + [const(a) for a in segs],
        out_specs=[row(QK_WIDTH), row(KV_LORA), row(LANES)],
        out_shape=[jax.ShapeDtypeStruct((bv, lv, QK_WIDTH), BF16),
                   jax.ShapeDtypeStruct((bv, lv, KV_LORA), F32),
                   jax.ShapeDtypeStruct((bv, lv, LANES), F32)],
        compiler_params=_cparams(("parallel", "parallel")),
        name="mla_q",
    )(zq, tabs[0], tabs[1], wts["q_norm"], wts["w_uq"], wts["w_uq_rot"], wts["g_q"], wts["g_q_rot"],
      wts["kv_norm"], wts["g_kr"], *segs)


def _expand_rows(lat_ref, krg_ref, ex_refs, k_ref, v_ref, src_rows, dst_rows):
    wk_ref, wv_ref, gk_ref, sel_ref, selt_ref, ilen_ref = ex_refs
    lat = lat_ref[0, src_rows, :].astype(BF16)
    kf = jnp.dot(lat, wk_ref[0], preferred_element_type=F32)
    scale = _segment_rms_scale(kf, sel_ref, selt_ref, ilen_ref)
    krg = jnp.concatenate([krg_ref[0, src_rows, :]] * B_HEADS, axis=1)
    k_ref[0, dst_rows, :] = (kf * scale * gk_ref[0] + krg).astype(BF16)
    lane = lax.broadcasted_iota(jnp.int32, (1, QK_WIDTH), 1)
    ones_col = jnp.where(lane % HEAD_PAD == V_DIM, 1.0, 0.0)
    v_ref[0, dst_rows, :] = (jnp.dot(lat, wv_ref[0], preferred_element_type=F32) + ones_col).astype(BF16)


def _expand_operands(wts, layer):
    segs = _segment_tables([(h * HEAD_PAD, h * HEAD_PAD + QK_NOPE) for h in range(B_HEADS)])
    par = lambda r, c: pl.BlockSpec((1, r, c), lambda *g: (layer, 0, 0))
    specs = [par(KV_LORA, QK_WIDTH), par(KV_LORA, QK_WIDTH), par(1, QK_WIDTH)]
    specs += [pl.BlockSpec(a.shape, lambda *g: (0, 0)) for a in segs]
    return specs, (wts["w_uk"], wts["w_uv"], wts["g_kn"], *segs)


def _attn_block(q_ref, k_ref, v_ref, start, tk, bias, m_ref, acc_ref):
    rows = pl.ds(start, tk)

    def scores(h):
        lanes = slice(h * HEAD_PAD, (h + 1) * HEAD_PAD)
        s = _mm_nt(q_ref[0, :, lanes], k_ref[0, rows, lanes])
        return s if bias is None else s + bias

    s_next = scores(0)
    for h in range(B_HEADS):
        s = s_next
        if h + 1 < B_HEADS:
            s_next = scores(h + 1)
        m_old = m_ref[h]
        m_new = jnp.maximum(m_old, jnp.max(s, axis=-1, keepdims=True))
        m_wide = m_new[:, :tk] if tk < LANES else jnp.concatenate([m_new] * (tk // LANES), axis=1)
        p = jnp.exp2(s - m_wide)
        pv = jnp.dot(p.astype(BF16), v_ref[0, rows, h * HEAD_PAD:(h + 1) * HEAD_PAD],
                     preferred_element_type=F32)
        acc_ref[h] = jnp.exp2(m_old - m_new) * acc_ref[h] + pv
        m_ref[h] = m_new


def _attn_finish(gb_ref, o_ref, acc_ref):
    outs = []
    for h in range(B_HEADS):
        acc = acc_ref[h]
        outs.append(acc[:, 0:V_DIM] / acc[:, V_DIM:V_DIM + 1])
    gb = gb_ref[0]
    o_ref[0] = (jnp.concatenate(outs, axis=1) * (gb * _sigmoid(gb))).astype(BF16)


def _attn_causal_kernel(q_ref, lat_ref, krg_ref, gb_ref, bias_ref, *rest, t):
    *ex_refs, o_ref, m_ref, acc_ref, k_ref, v_ref = rest
    qi = pl.program_id(1)

    @pl.when(qi == 0)
    def _():
        def expand(r, carry):
            rows = pl.ds(pl.multiple_of(r * t, t), t)
            _expand_rows(lat_ref, krg_ref, ex_refs, k_ref, v_ref, rows, rows)
            return carry

        lax.fori_loop(0, lat_ref.shape[1] // t, expand, 0)

    m_ref[...] = jnp.full(m_ref.shape, NEG_INF, F32)
    acc_ref[...] = jnp.zeros(acc_ref.shape, F32)
    block = functools.partial(_attn_block, q_ref, k_ref, v_ref, tk=t, m_ref=m_ref, acc_ref=acc_ref)
    block(start=0, bias=bias_ref[jnp.where(qi == 0, 2, 1)])

    def kv_step(c, carry):
        block(start=pl.multiple_of(c * t, t), bias=None)
        return carry

    lax.fori_loop(1, qi, kv_step, 0)

    @pl.when(qi > 0)
    def _():
        block(start=pl.multiple_of(qi * t, t), bias=bias_ref[0])

    _attn_finish(gb_ref, o_ref, acc_ref)


def _attention_causal(q, lat, krg, gb, bias, wts, layer, t):
    bs, lq, _ = q.shape
    whole = lambda w: pl.BlockSpec((1, lq, w), lambda b, i: (b, 0, 0))
    ex_specs, ex_args = _expand_operands(wts, layer)
    return pl.pallas_call(
        functools.partial(_attn_causal_kernel, t=t),
        grid=(bs, lq // t),
        in_specs=[pl.BlockSpec((1, t, QK_WIDTH), lambda b, i: (b, i, 0)),
                  whole(KV_LORA), whole(LANES),
                  pl.BlockSpec((1, t, B_WIDTH), lambda b, i: (b, i, 0)),
                  pl.BlockSpec(bias.shape, lambda b, i: (0, 0, 0), pipeline_mode=pl.Buffered(1))] + ex_specs,
        out_specs=pl.BlockSpec((1, t, B_WIDTH), lambda b, i: (b, i, 0)),
        out_shape=jax.ShapeDtypeStruct((bs, lq, B_WIDTH), BF16),
        scratch_shapes=[pltpu.VMEM((B_HEADS, t, LANES), F32), pltpu.VMEM((B_HEADS, t, HEAD_PAD), F32),
                        pltpu.VMEM((1, lq, QK_WIDTH), BF16), pltpu.VMEM((1, lq, QK_WIDTH), BF16)],
        compiler_params=_cparams(("parallel", "arbitrary")),
        name="attention",
    )(q, lat, krg, gb, bias, *ex_args)


def _prompt_bias(t, n_invisible):
    i = np.arange(t)
    diag = np.where((i[None, :] // CHUNK) <= (i[:, None] // CHUNK), 0.0, NEG_INF)
    first = np.broadcast_to(np.where(i[None, :] >= n_invisible, 0.0, NEG_INF), (t, t))
    return jnp.asarray(np.stack([diag, first, np.minimum(diag, first)]), F32)


def _attn_cached_kernel(q_ref, latp_ref, krgp_ref, latn_ref, krgn_ref, gb_ref, *rest, rows_per_step):
    *ex_refs, o_ref, m_ref, acc_ref, kp_ref, vp_ref, kn_ref, vn_ref = rest
    n_past, n_new = latp_ref.shape[2], latn_ref.shape[1]
    for r in range(n_past // rows_per_step):
        rows = pl.ds(r * rows_per_step, rows_per_step)
        _expand_rows(latp_ref.at[0], krgp_ref.at[0], ex_refs, kp_ref, vp_ref, rows, rows)
    _expand_rows(latn_ref, krgn_ref, ex_refs, kn_ref, vn_ref, pl.ds(0, n_new), pl.ds(0, n_new))
    m_ref[...] = jnp.full(m_ref.shape, NEG_INF, F32)
    acc_ref[...] = jnp.zeros(acc_ref.shape, F32)
    _attn_block(q_ref, kp_ref, vp_ref, 0, n_past, None, m_ref, acc_ref)
    _attn_block(q_ref, kn_ref, vn_ref, 0, n_new, None, m_ref, acc_ref)
    _attn_finish(gb_ref, o_ref, acc_ref)


def _attention_cached(q, lat_past, krg_past, lat_new, krg_new, gb, wts, layer):
    bs, lq, _ = q.shape
    n_past = lat_past.shape[2]
    seq = lambda n, w: pl.BlockSpec((1, n, w), lambda b: (b, 0, 0))
    cached = lambda w: pl.BlockSpec((1, 1, n_past, w), lambda b: (layer, b, 0, 0))
    ex_specs, ex_args = _expand_operands(wts, layer)
    return pl.pallas_call(
        functools.partial(_attn_cached_kernel, rows_per_step=_largest_tile(n_past, 512, LANES)),
        grid=(bs,),
        in_specs=[seq(lq, QK_WIDTH), cached(KV_LORA), cached(LANES), seq(lq, KV_LORA), seq(lq, LANES),
                  seq(lq, B_WIDTH)] + ex_specs,
        out_specs=seq(lq, B_WIDTH),
        out_shape=jax.ShapeDtypeStruct((bs, lq, B_WIDTH), BF16),
        scratch_shapes=[pltpu.VMEM((B_HEADS, lq, LANES), F32), pltpu.VMEM((B_HEADS, lq, HEAD_PAD), F32),
                        pltpu.VMEM((1, n_past, QK_WIDTH), BF16), pltpu.VMEM((1, n_past, QK_WIDTH), BF16),
                        pltpu.VMEM((1, lq, QK_WIDTH), BF16), pltpu.VMEM((1, lq, QK_WIDTH), BF16)],
        compiler_params=_cparams(("parallel",)),
        name="attention_cached",
    )(q, lat_past, krg_past, lat_new, krg_new, gb, *ex_args)


def _out_proj_kernel(x_ref, ya_ref, yb_ref, zg_ref, wa_ref, wb_ref, wo_ref, o_ref):
    zg = zg_ref[0]
    ga = _sigmoid(zg[:, 0:D_MODEL])
    gb = _sigmoid(zg[:, D_MODEL:G_COLS])
    t = (ga * jnp.dot(ya_ref[0], wa_ref[0], preferred_element_type=F32)
         + gb * jnp.dot(yb_ref[0], wb_ref[0], preferred_element_type=F32))
    o_ref[0] = x_ref[0] + _mm(t, wo_ref[0])


def _out_proj(x, ya, yb, zg, wts, layer, tm):
    bv, lv, _ = x.shape
    row = lambda w: pl.BlockSpec((1, tm, w), lambda b, j: (b, j, 0))
    par = lambda r, c: pl.BlockSpec((1, r, c), lambda b, j: (layer, 0, 0))
    return pl.pallas_call(
        _out_proj_kernel,
        grid=(bv, lv // tm),
        in_specs=[row(D_MODEL), row(A_WIDTH), row(B_WIDTH), row(G_COLS),
                  par(A_WIDTH, D_MODEL), par(B_WIDTH, D_MODEL), par(D_MODEL, D_MODEL)],
        out_specs=row(D_MODEL),
        out_shape=jax.ShapeDtypeStruct((bv, lv, D_MODEL), F32),
        compiler_params=_cparams(("parallel", "parallel")),
        name="out_proj",
    )(x, ya, yb, zg, wts["w_a"], wts["w_b"], wts["w_o"])


def _prep_weights(w_in, shift_mix, rwkv_w0, rwkv_w2, rwkv_a0, rwkv_a2, rwkv_k_k, rwkv_k_a, rwkv_r_k,
                  rwkv_ln_w, rwkv_ln_b, mla_q_norm, mla_w_uq, mla_kv_norm, mla_w_ukv, mla_qn_nope,
                  mla_kn_nope, mla_qn_rope, mla_kn_rope, w_branch_a, w_branch_b, w_out):
    depth = w_in.shape[0]
    a_end, b0 = A_COLS, A_COLS
    qc = w_in[:, :, b0:b0 + Q_LORA + KV_LORA]
    kr = w_in[:, :, b0 + Q_LORA + KV_LORA:b0 + Q_LORA + KV_LORA + QK_ROPE]
    gate_b = w_in[:, :, b0 + Q_LORA + KV_LORA + QK_ROPE:b0 + B_COLS]
    kr_grp = jnp.pad(kr, ((0, 0), (0, 0), (ROPE_LO, LANES - ROPE_LO - QK_ROPE)))
    w_in_p = jnp.concatenate([w_in[:, :, :a_end], qc, kr_grp, gate_b, w_in[:, :, A_COLS + B_COLS:]],
                             axis=-1).astype(BF16)
    zeros = jnp.zeros((depth, DECAY_LORA, A_WIDTH), F32)
    lora = jnp.concatenate([jnp.concatenate([rwkv_w2, zeros], axis=2),
                            jnp.concatenate([zeros, rwkv_a2], axis=2)], axis=1).astype(BF16)
    row = lambda p: p.reshape(depth, 1, -1)
    w_uq = mla_w_uq.reshape(depth, Q_LORA, B_HEADS, QK_DIM)
    w_uq = jnp.pad(w_uq, ((0, 0), (0, 0), (0, 0), (0, HEAD_PAD - QK_DIM)))
    g_q = jnp.pad(jnp.concatenate([mla_qn_nope, mla_qn_rope], axis=1), ((0, 0), (0, HEAD_PAD - QK_DIM)))
    half = QK_ROPE // 2

    def partner(a):
        lo, hi = a[..., ROPE_LO:ROPE_LO + half], a[..., ROPE_LO + half:ROPE_LO + QK_ROPE]
        zero = jnp.zeros_like(a[..., :ROPE_LO])
        return jnp.concatenate([zero, hi, lo, zero[..., :HEAD_PAD - ROPE_LO - QK_ROPE]], axis=-1)

    q_scale = QK_DIM ** -0.5 * float(np.log2(np.e))
    heads = lambda g: jnp.tile(g, (1, B_HEADS))
    w_uq_rot = partner(w_uq).reshape(depth, Q_LORA, QK_WIDTH)
    w_uq = w_uq.reshape(depth, Q_LORA, QK_WIDTH)
    g_q_rot = heads(partner(g_q)) * q_scale
    g_q = heads(g_q) * q_scale
    g_kr = jnp.pad(mla_kn_rope, ((0, 0), (ROPE_LO, LANES - ROPE_LO - QK_ROPE)))
    g_kn = heads(jnp.pad(mla_kn_nope, ((0, 0), (0, HEAD_PAD - QK_NOPE))))
    w_ukv = mla_w_ukv.reshape(depth, KV_LORA, B_HEADS, QK_NOPE + V_DIM)
    w_uk = jnp.pad(w_ukv[..., :QK_NOPE], ((0, 0), (0, 0), (0, 0), (0, HEAD_PAD - QK_NOPE)))
    w_uv = jnp.pad(w_ukv[..., QK_NOPE:], ((0, 0), (0, 0), (0, 0), (0, HEAD_PAD - V_DIM)))
    return dict(
        w_in=w_in_p, shift_mix=shift_mix, w0=row(rwkv_w0), a0=row(rwkv_a0), lora=lora,
        k_k=row(rwkv_k_k), k_a=row(rwkv_k_a), r_k=row(rwkv_r_k), ln_w=row(rwkv_ln_w), ln_b=row(rwkv_ln_b),
        q_norm=row(mla_q_norm), w_uq=w_uq.astype(BF16), w_uq_rot=w_uq_rot.astype(BF16), g_q=row(g_q),
        g_q_rot=row(g_q_rot), kv_norm=row(mla_kv_norm),
        g_kr=row(g_kr), g_kn=row(g_kn),
        w_uk=w_uk.reshape(depth, KV_LORA, QK_WIDTH).astype(BF16),
        w_uv=w_uv.reshape(depth, KV_LORA, QK_WIDTH).astype(BF16),
        w_a=w_branch_a.astype(BF16), w_b=w_branch_b.astype(BF16), w_o=w_out.astype(BF16))


def _rope_tables(t):
    half = QK_ROPE // 2
    inv = ROPE_THETA ** (-jnp.arange(0, QK_ROPE, 2, dtype=F32) / QK_ROPE)
    ang = t.astype(F32)[:, None] * inv[None, :]
    cos, sin = jnp.cos(ang), jnp.sin(ang)
    n = t.shape[0]
    ones = lambda w: jnp.ones((n, w), F32)
    zeros = lambda w: jnp.zeros((n, w), F32)
    tail = LANES - ROPE_LO - QK_ROPE
    cos_t = jnp.concatenate([ones(ROPE_LO), cos, cos, ones(tail)], axis=1)
    sin_t = jnp.concatenate([zeros(ROPE_LO), -sin, sin, zeros(tail)], axis=1)
    return cos_t, sin_t


def _largest_tile(n, cap, mult):
    assert n % mult == 0, (n, mult)
    best = mult
    for t in range(mult, min(n, cap) + 1, mult):
        if n % t == 0:
            best = t
    return best


def _layer(x, layer, wts, tabs, cfg, shift_row, s0, past=None):
    bv, lv, _ = x.shape
    bs, ls = cfg["bs"], cfg["ls"]
    seq = lambda a: a.reshape(bs, ls, a.shape[-1])
    za, zq, zgb, zg = _in_proj(x, wts["norm_w"], wts["w_in"], layer, cfg["tm"], cfg["n_pad"])
    za_s = seq(za)
    prep = _rwkv_prep(za_s, shift_row, wts, layer, cfg["tm_prep"], cfg["chunk"])
    ya, s_fin = _rwkv_chain(prep, s0, wts, layer, cfg["nb"], cfg["nc"], cfg["chunk"], cfg["g"])
    q, lat, krg = _mla_q(zq, tabs, wts, layer, cfg["tm"])
    if past is None:
        yb = _attention_causal(seq(q), seq(lat), seq(krg), seq(zgb), cfg["bias"], wts, layer, cfg["tq"])
    else:
        yb = _attention_cached(seq(q), past[0], past[1], seq(lat), seq(krg), seq(zgb), wts, layer)
    x_new = _out_proj(x, ya.reshape(bv, lv, A_WIDTH), yb.reshape(bv, lv, B_WIDTH), zg, wts, layer, cfg["tm"])
    return x_new, s_fin, za_s[:, -1:, :], seq(lat), seq(krg)[:, :, ROPE_LO:ROPE_LO + QK_ROPE]


def kernel(x_prompt, x_sample, state_rwkv, state_shift, cache_mla_latent, cache_mla_krope, meta_tokens, norm_w, w_in, shift_mix, rwkv_w0, rwkv_w2, rwkv_a0, rwkv_a2, rwkv_k_k, rwkv_k_a, rwkv_r_k, rwkv_ln_w, rwkv_ln_b, mla_q_norm, mla_w_uq, mla_kv_norm, mla_w_ukv, mla_qn_nope, mla_kn_nope, mla_qn_rope, mla_kn_rope, w_branch_a, w_branch_b, w_out):
    depth = w_in.shape[0]
    bp, seq_len, _ = x_prompt.shape
    bd, dec_len, _ = x_sample.shape
    past_len = cache_mla_latent.shape[2]
    assert seq_len % LANES == 0 and dec_len % 8 == 0
    wts = _prep_weights(w_in, shift_mix, rwkv_w0, rwkv_w2, rwkv_a0, rwkv_a2, rwkv_k_k, rwkv_k_a, rwkv_r_k,
                        rwkv_ln_w, rwkv_ln_b, mla_q_norm, mla_w_uq, mla_kv_norm, mla_w_ukv, mla_qn_nope,
                        mla_kn_nope, mla_qn_rope, mla_kn_rope, w_branch_a, w_branch_b, w_out)
    wts["norm_w"] = norm_w.reshape(depth, 1, D_MODEL)

    n_zero = PAD_FRONT - N_META
    lp = PAD_FRONT + seq_len
    meta = jnp.broadcast_to(meta_tokens[None].astype(F32), (bp, N_META, D_MODEL))
    xp = jnp.concatenate([jnp.zeros((bp, n_zero, D_MODEL), F32), meta, x_prompt], axis=1)
    tabs_p = _rope_tables(jnp.arange(lp, dtype=jnp.int32) - PAD_FRONT)
    tile_p = _largest_tile(lp, 512, LANES)
    cfg_p = dict(bs=bp, ls=lp, tm=tile_p, n_pad=n_zero, tm_prep=_largest_tile(lp, 256, CHUNK),
                 chunk=CHUNK, g=4, nb=4, nc=1, tq=tile_p, bias=_prompt_bias(tile_p, n_zero))
    zero_row = jnp.zeros((bp, 1, A_COLS), F32)
    zero_state = jnp.zeros((bp, A_HEADS, A_HEAD_DIM, A_HEAD_DIM), F32)

    n_rows = bd * dec_len
    xs = x_sample.reshape(1, n_rows, D_MODEL)
    t_s = past_len + (jnp.arange(n_rows, dtype=jnp.int32) % dec_len)
    tabs_s = _rope_tables(t_s)
    cfg_s = dict(bs=bd, ls=dec_len, tm=_largest_tile(n_rows, 512, 8), n_pad=0, tm_prep=dec_len,
                 chunk=dec_len, g=A_HEADS, nb=4, nc=1, tq=dec_len)

    past_lat = cache_mla_latent
    past_krg = jnp.pad(cache_mla_krope, ((0, 0), (0, 0), (0, 0), (ROPE_LO, LANES - ROPE_LO - QK_ROPE)))

    outs_p, outs_s = [], []
    for l in range(depth):
        xp, *op = _layer(xp, l, wts, tabs_p, cfg_p, zero_row, zero_state)
        xs, *os_ = _layer(xs, l, wts, tabs_s, cfg_s, state_shift[l], state_rwkv[l], past=(past_lat, past_krg))
        outs_p.append(op)
        outs_s.append(os_)
    first = PAD_FRONT - N_META
    stack = lambda outs, i, lo=0: jnp.stack([o[i][:, lo:] for o in outs])
    return (xp[:, PAD_FRONT:], xs.reshape(bd, dec_len, D_MODEL),
            stack(outs_p, 0), stack(outs_p, 1), stack(outs_p, 2, first), stack(outs_p, 3, first),
            stack(outs_s, 0), stack(outs_s, 1), stack(outs_s, 2), stack(outs_s, 3))
```

```python
import functools

import numpy as np
import jax
import jax.numpy as jnp
from jax import lax
from jax.experimental import pallas as pl
from jax.experimental.pallas import tpu as pltpu

F32 = jnp.float32
BF16 = jnp.bfloat16

D_MODEL = 1024
CHUNK = 64
N_META = 16
A_HEADS = 8
A_HEAD_DIM = 64
A_WIDTH = A_HEADS * A_HEAD_DIM
DECAY_LORA = 64
ICLR_LORA = 64
B_HEADS = 8
QK_NOPE = 64
QK_ROPE = 32
QK_DIM = QK_NOPE + QK_ROPE
V_DIM = 64
B_WIDTH = B_HEADS * V_DIM
Q_LORA = 384
KV_LORA = 256
ROPE_THETA = 10000.0
NORM_EPS = 1e-6
GN_EPS = 64e-5
NEG_INF = -1e30
A_COLS = 4 * A_WIDTH + DECAY_LORA + ICLR_LORA
B_COLS = Q_LORA + KV_LORA + QK_ROPE + B_WIDTH
G_COLS = 2 * D_MODEL

LANES = 128
HEAD_PAD = LANES
QK_WIDTH = B_HEADS * HEAD_PAD
ZQ_COLS = Q_LORA + KV_LORA + LANES
ROPE_LO = QK_NOPE
PAD_FRONT = 128
VMEM_LIMIT = 56 * 1024 * 1024
BASE_BLOCK = 8


def _cparams(sem):
    return pltpu.CompilerParams(dimension_semantics=sem, vmem_limit_bytes=VMEM_LIMIT)


def _mm(a, b):
    return jnp.dot(a.astype(BF16), b.astype(BF16), preferred_element_type=F32)


def _mm_nt(a, b):
    return lax.dot_general(a.astype(BF16), b.astype(BF16), (((1,), (1,)), ((), ())),
                           preferred_element_type=F32)


def _mm_tn(a, b):
    return lax.dot_general(a.astype(BF16), b.astype(BF16), (((0,), (0,)), ((), ())),
                           preferred_element_type=F32)


def _split2(x):
    hi = x.astype(BF16)
    return hi, (x - hi.astype(F32)).astype(BF16)


def _mm_x_exact(x, e):
    hi, lo = _split2(x)
    return jnp.dot(hi, e, preferred_element_type=F32) + jnp.dot(lo, e, preferred_element_type=F32)


def _mm_exact_x(e, x):
    hi, lo = _split2(x)
    return jnp.dot(e, hi, preferred_element_type=F32) + jnp.dot(e, lo, preferred_element_type=F32)


def _sigmoid(x):
    return 1.0 / (1.0 + jnp.exp(-x))


def _in_proj_kernel(x_ref, nw_ref, w_ref, za_ref, zq_ref, zgb_ref, zg_ref, *, tm, n_pad):
    x = x_ref[0]
    ms = jnp.mean(x * x, axis=-1, keepdims=True)
    h = x * lax.rsqrt(ms + NORM_EPS) * nw_ref[0]
    if n_pad:
        row = pl.program_id(1) * tm + lax.broadcasted_iota(jnp.int32, (tm, 1), 0)
        h = jnp.where(row >= n_pad, h, 0.0)
    hb = h.astype(BF16)
    o = 0
    for ref, width in ((za_ref, A_COLS), (zq_ref, ZQ_COLS), (zgb_ref, B_WIDTH), (zg_ref, G_COLS)):
        ref[0] = jnp.dot(hb, w_ref[0, :, o:o + width], preferred_element_type=F32)
        o += width


def _in_proj(x, norm_w, w_in_p, layer, tm, n_pad):
    bv, lv, _ = x.shape
    cols = w_in_p.shape[-1]
    row = lambda w: pl.BlockSpec((1, tm, w), lambda b, j: (b, j, 0))
    return pl.pallas_call(
        functools.partial(_in_proj_kernel, tm=tm, n_pad=n_pad),
        grid=(bv, lv // tm),
        in_specs=[row(D_MODEL),
                  pl.BlockSpec((1, 1, D_MODEL), lambda b, j: (layer, 0, 0)),
                  pl.BlockSpec((1, D_MODEL, cols), lambda b, j: (layer, 0, 0))],
        out_specs=[row(A_COLS), row(ZQ_COLS), row(B_WIDTH), row(G_COLS)],
        out_shape=[jax.ShapeDtypeStruct((bv, lv, w), F32) for w in (A_COLS, ZQ_COLS, B_WIDTH, G_COLS)],
        compiler_params=_cparams(("parallel", "parallel")),
        name="in_proj",
    )(x, norm_w, w_in_p)


def _rwkv_prep_kernel(za_ref, zap_ref, sh_ref, mix_ref, w0_ref, a0_ref, lora_ref, kk_ref, ka_ref,
                      rk_ref, e_ref, tri_ref,
                      at_ref, rt_ref, bt_ref, kt_ref, vb_ref, pc_ref, bonus_ref, sg_ref, *, tm, chunk):
    j = pl.program_id(1)
    a = za_ref[0]
    first = jnp.where(j == 0, sh_ref[0], zap_ref[0, 7:8, :])
    rowid = lax.broadcasted_iota(jnp.int32, (tm, 1), 0)
    prev = jnp.where(rowid == 0, first, pltpu.roll(a, 1, 0))
    xs = a * mix_ref[0, 0:1, :] + prev * mix_ref[0, 1:2, :]
    r = xs[:, 0:A_WIDTH]
    k = xs[:, A_WIDTH:2 * A_WIDTH]
    v = xs[:, 2 * A_WIDTH:3 * A_WIDTH]
    gate = xs[:, 3 * A_WIDTH:4 * A_WIDTH]
    lr = xs[:, 4 * A_WIDTH:A_COLS]
    lane = lax.broadcasted_iota(jnp.int32, (1, DECAY_LORA + ICLR_LORA), 1)
    lr = jnp.where(lane < DECAY_LORA, jnp.tanh(lr), lr)
    lo = _mm(lr, lora_ref[0])
    wlin = w0_ref[0] + lo[:, 0:A_WIDTH]
    alin = a0_ref[0] + lo[:, A_WIDTH:2 * A_WIDTH]
    logdec = -float(np.exp(-0.5)) * _sigmoid(wlin)
    av = _sigmoid(alin)
    e = e_ref[...]
    kk = k * kk_ref[0]
    kk = kk * lax.rsqrt(jnp.maximum(_mm(kk * kk, e), 1e-24))
    kmod = k * (1.0 + (av - 1.0) * ka_ref[0])
    sums = _mm_exact_x(tri_ref[...], logdec)
    cum, tot = sums[:tm], sums[tm:]
    pin = jnp.exp(cum)
    pex = jnp.exp(cum - logdec)
    pinv = jnp.exp(-cum)
    at_ref[0] = (-kk * pex).astype(BF16)
    rt_ref[0] = (r * pin).astype(BF16)
    bt_ref[0] = (kk * av * pinv).astype(BF16)
    kt_ref[0] = (kmod * pinv).astype(BF16)
    vb_ref[0] = v.astype(BF16)
    pc = jnp.exp(tot)
    pc_ref[0] = jnp.concatenate([pc[i * chunk:i * chunk + 8] for i in range(tm // chunk)], axis=0)
    bonus_ref[0] = (_mm_x_exact(r * kmod * rk_ref[0], e) * v).astype(BF16)
    sg_ref[0] = (gate * _sigmoid(gate)).astype(BF16)


def _rwkv_prep(za, shift_row, wts, layer, tm, chunk):
    bs, ls, _ = za.shape
    ids = np.arange(tm)
    same = (ids[:, None] // chunk) == (ids[None, :] // chunk)
    tri = same & (ids[None, :] <= ids[:, None])
    e = (np.arange(A_WIDTH)[:, None] // A_HEAD_DIM) == (np.arange(A_WIDTH)[None, :] // A_HEAD_DIM)
    row = lambda w: pl.BlockSpec((1, tm, w), lambda b, j: (b, j, 0))
    par = lambda r, c: pl.BlockSpec((1, r, c), lambda b, j: (layer, 0, 0))
    const = lambda r, c: pl.BlockSpec((r, c), lambda b, j: (0, 0))
    nsub = tm // 8
    seq_bf16 = jax.ShapeDtypeStruct((bs, ls, A_WIDTH), BF16)
    pc_rows = 8 * tm // chunk
    outs = [seq_bf16] * 5 + [jax.ShapeDtypeStruct((bs, 8 * ls // chunk, A_WIDTH), F32)] + [seq_bf16] * 2
    return pl.pallas_call(
        functools.partial(_rwkv_prep_kernel, tm=tm, chunk=chunk),
        grid=(bs, ls // tm),
        in_specs=[row(A_COLS),
                  pl.BlockSpec((1, 8, A_COLS), lambda b, j: (b, jnp.maximum(j * nsub - 1, 0), 0)),
                  pl.BlockSpec((1, 1, A_COLS), lambda b, j: (b, 0, 0)),
                  par(2, A_COLS), par(1, A_WIDTH), par(1, A_WIDTH),
                  par(DECAY_LORA + ICLR_LORA, 2 * A_WIDTH),
                  par(1, A_WIDTH), par(1, A_WIDTH), par(1, A_WIDTH),
                  const(A_WIDTH, A_WIDTH), const(2 * tm, tm)],
        out_specs=[row(A_WIDTH)] * 5 + [pl.BlockSpec((1, pc_rows, A_WIDTH), lambda b, j: (b, j, 0))]
                  + [row(A_WIDTH)] * 2,
        out_shape=outs,
        compiler_params=_cparams(("parallel", "parallel")),
        name="rwkv_prep",
    )(za, za, shift_row, wts["shift_mix"], wts["w0"], wts["a0"], wts["lora"], wts["k_k"], wts["k_a"],
      wts["r_k"], jnp.asarray(e, BF16), jnp.asarray(np.concatenate([tri, same], axis=0), BF16))


def _chain_masks(chunk, g):
    gc, w = g * chunk, g * A_HEAD_DIM
    t = np.arange(chunk)[:, None]
    s = np.arange(gc)[None, :] % chunk
    wide = {"strict": s < t, "incl": s <= t, "eye": s == t,
            "base": (s // BASE_BLOCK == t // BASE_BLOCK) & (s < t)}
    levels = []
    sz = BASE_BLOCK
    while sz < chunk:
        levels.append((s // (2 * sz) == t // (2 * sz)) & (s // sz != t // sz) & (s < t))
        sz *= 2
    rows = np.arange(gc)[:, None] // chunk
    bd_nat = rows == (np.arange(w)[None, :] // A_HEAD_DIM)
    bd_wide = rows == (np.arange(gc)[None, :] // chunk)
    bd_state = (np.arange(w)[:, None] // A_HEAD_DIM) == (np.arange(w)[None, :] // A_HEAD_DIM)
    f = lambda m: jnp.asarray(m, F32)
    b = lambda m: jnp.asarray(m, BF16)
    lev = np.stack(levels) if levels else np.zeros((1, chunk, gc), bool)
    return dict(strict=f(wide["strict"]), incl=f(wide["incl"]), eye=f(wide["eye"]), base=f(wide["base"]),
                lev=f(lev), bd_nat=b(bd_nat), bd_wide=b(bd_wide), bd_state=f(bd_state)), len(levels)


def _rwkv_chain_kernel(at_ref, rt_ref, bt_ref, kt_ref, vb_ref, pc_ref, bonus_ref, sg_ref, s0_ref,
                       lnw_ref, lnb_ref, e_ref, strict_ref, incl_ref, eye_ref, base_ref, lev_ref,
                       bdn_ref, bdw_ref, bds_ref, rep_ref, rept_ref, ya_ref, sout_ref, s_ref, *,
                       nb, nc, chunk, g, n_lev):
    c = chunk
    gc, w = g * c, g * A_HEAD_DIM
    n_groups = A_HEADS // g

    @pl.when(pl.program_id(1) == 0)
    def _():
        for b in range(nb):
            for grp in range(n_groups):
                stacked = s0_ref[b, grp * g:(grp + 1) * g].reshape(w, A_HEAD_DIM)
                s_ref[b, grp] = _mm_x_exact(stacked, rep_ref[...]) * bds_ref[...]

    strict, incl, eye, base = strict_ref[...], incl_ref[...], eye_ref[...], base_ref[...]
    bdn, bdw, bds = bdn_ref[...], bdw_ref[...], bds_ref[...]

    def bd(x, mask):
        return jnp.concatenate([x.astype(BF16)] * g, axis=0) * mask

    def wide_mm(x, y):
        return jnp.dot(x.astype(BF16), bd(y, bdw), preferred_element_type=F32)

    chains = [(b, grp) for b in range(nb) for grp in range(n_groups)]
    lanes = [slice(grp * w, (grp + 1) * w) for _, grp in chains]
    each = lambda fn, *lists: [fn(*args) for args in zip(*lists)]
    dot = lambda x, y: jnp.dot(x.astype(BF16), y, preferred_element_type=F32)
    cat0 = lambda x, y: jnp.concatenate([x, y], axis=0)
    states = [s_ref[b, grp] for b, grp in chains]
    y_chunks = [[] for _ in chains]
    for ci in range(nc):
        rows = slice(ci * c, (ci + 1) * c)
        load = lambda ref: [ref[b, rows, sl] for (b, _), sl in zip(chains, lanes)]
        at, rt, bt, kt, vb = (load(ref) for ref in (at_ref, rt_ref, bt_ref, kt_ref, vb_ref))
        ar = each(cat0, at, rt)
        gm = each(lambda a, b_, k_: _mm_nt(a, cat0(bd(b_, bdn), bd(k_, bdn))), ar, bt, kt)
        m_ab = [g_[:c, :gc] * strict for g_ in gm]
        d1 = [m * base for m in m_ab]
        d2 = each(wide_mm, d1, d1)
        t = [eye + d for d in d1]
        t = each(lambda t_, d: t_ + wide_mm(t_, d), t, d2)
        d4 = each(wide_mm, d2, d2)
        t = each(lambda t_, d: t_ + wide_mm(t_, d), t, d4)
        for lv in range(n_lev):
            off = each(lambda m, t_: wide_mm(m * lev_ref[lv], t_), m_ab, t)
            t = each(lambda t_, o: t_ + wide_mm(t_, o), t, off)
        vbd = [bd(v, bdn) for v in vb]
        akv = each(lambda g_, v: dot(g_[:c, gc:] * strict, v), gm, vbd)
        ah = each(_mm_nt, ar, states)
        u = each(lambda t_, a, k_: dot(t_, bd(a[:c] + k_, bdn)).astype(BF16), t, ah, akv)
        y = each(lambda a, g_, u_, v: a[c:] + dot(g_[c:] * jnp.concatenate([incl, incl], axis=1),
                                                  cat0(bd(u_, bdn), v)), ah, gm, u, vbd)
        upd = each(lambda u_, v, b_, k_: _mm_tn(cat0(u_, v), cat0(b_, k_)), u, vb, bt, kt)
        states = [(s + up * bds) * pc_ref[b, ci * 8:ci * 8 + 1, sl]
                  for s, up, (b, _), sl in zip(states, upd, chains, lanes)]
        for yc, y_ in zip(y_chunks, y):
            yc.append(y_)
    for (b, grp), s in zip(chains, states):
        s_ref[b, grp] = s
    seqs = [yc[0] if nc == 1 else jnp.concatenate(yc, axis=0) for yc in y_chunks]
    ys = [seqs[b * n_groups] if n_groups == 1 else jnp.concatenate(seqs[b * n_groups:(b + 1) * n_groups], axis=1)
          for b in range(nb)]
    y = ys[0] if nb == 1 else jnp.concatenate(ys, axis=0)
    e = e_ref[...]
    inv_n = 1.0 / A_HEAD_DIM
    mu = _mm_x_exact(y, e) * inv_n
    d = y - mu
    var = _mm(d * d, e) * inv_n
    yn = d * lax.rsqrt(var + GN_EPS) * lnw_ref[0] + lnb_ref[0]
    bonus = bonus_ref[...].reshape(nb * nc * c, A_WIDTH)
    sg = sg_ref[...].reshape(nb * nc * c, A_WIDTH)
    ya_ref[...] = ((yn + bonus) * sg).astype(BF16).reshape(nb, nc * c, A_WIDTH)

    @pl.when(pl.program_id(1) == pl.num_programs(1) - 1)
    def _():
        for (b, grp), s in zip(chains, states):
            blocks = _mm_x_exact(s, rept_ref[...])
            sout_ref[b, grp * g:(grp + 1) * g] = blocks.reshape(g, A_HEAD_DIM, A_HEAD_DIM)


def _rwkv_chain(prep, s0, wts, layer, nb, nc, chunk, g):
    at, rt, bt, kt, vb, pc, bonus, sg = prep
    bs, ls, _ = at.shape
    gc, w = g * chunk, g * A_HEAD_DIM
    n_groups = A_HEADS // g
    masks, n_lev = _chain_masks(chunk, g)
    e = (np.arange(A_WIDTH)[:, None] // A_HEAD_DIM) == (np.arange(A_WIDTH)[None, :] // A_HEAD_DIM)
    row = pl.BlockSpec((nb, nc * chunk, A_WIDTH), lambda b, j: (b, j, 0))
    par = pl.BlockSpec((1, 1, A_WIDTH), lambda b, j: (layer, 0, 0))
    state = pl.BlockSpec((nb, A_HEADS, A_HEAD_DIM, A_HEAD_DIM), lambda b, j: (b, 0, 0, 0))
    const = lambda shape: pl.BlockSpec(shape, lambda b, j: (0,) * len(shape))
    rep = np.tile(np.eye(A_HEAD_DIM), (1, g))
    return pl.pallas_call(
        functools.partial(_rwkv_chain_kernel, nb=nb, nc=nc, chunk=chunk, g=g, n_lev=n_lev),
        grid=(bs // nb, ls // (nc * chunk)),
        in_specs=[row] * 5 + [pl.BlockSpec((nb, 8 * nc, A_WIDTH), lambda b, j: (b, j, 0)), row, row]
                 + [state, par, par, const((A_WIDTH, A_WIDTH)),
                              const((chunk, gc)), const((chunk, gc)), const((chunk, gc)), const((chunk, gc)),
                              const(tuple(masks["lev"].shape)),
                              const((gc, w)), const((gc, gc)), const((w, w)),
                              const((A_HEAD_DIM, w)), const((w, A_HEAD_DIM))],
        out_specs=[row, state],
        out_shape=[jax.ShapeDtypeStruct((bs, ls, A_WIDTH), BF16),
                   jax.ShapeDtypeStruct((bs, A_HEADS, A_HEAD_DIM, A_HEAD_DIM), F32)],
        scratch_shapes=[pltpu.VMEM((nb, n_groups, w, w), F32)],
        compiler_params=_cparams(("parallel", "arbitrary")),
        name="rwkv_chain",
    )(at, rt, bt, kt, vb, pc, bonus, sg, s0, wts["ln_w"], wts["ln_b"], jnp.asarray(e, BF16),
      masks["strict"], masks["incl"], masks["eye"], masks["base"], masks["lev"],
      masks["bd_nat"], masks["bd_wide"], masks["bd_state"], jnp.asarray(rep, BF16), jnp.asarray(rep.T, BF16))


def _segment_tables(bounds):
    sel = np.zeros((QK_WIDTH, LANES), np.float32)
    inv_len = np.ones((1, LANES), np.float32)
    for col, (lo, hi) in enumerate(bounds):
        sel[lo:hi, col] = 1.0
        inv_len[0, col] = 1.0 / (hi - lo)
    return jnp.asarray(sel, BF16), jnp.asarray(sel.T, BF16), jnp.asarray(inv_len, F32)


def _segment_rms_scale(x, sel_ref, selt_ref, ilen_ref):
    ss = _mm(x * x, sel_ref[...])
    return _mm(lax.rsqrt(ss * ilen_ref[...] + NORM_EPS), selt_ref[...])


def _mla_q_kernel(zq_ref, cos_ref, sin_ref, qn_ref, wuq_ref, wrot_ref, gq_ref, gqr_ref, kvn_ref, gk_ref,
                  sel_ref, selt_ref, ilen_ref, q_ref, lat_ref, krg_ref):
    zq = zq_ref[0]
    qc = zq[:, 0:Q_LORA]
    ckv = zq[:, Q_LORA:Q_LORA + KV_LORA]
    kr = zq[:, Q_LORA + KV_LORA:ZQ_COLS]
    rms = lambda x, n: x * lax.rsqrt(jnp.sum(x * x, axis=-1, keepdims=True) * (1.0 / n) + NORM_EPS)
    qn = (rms(qc, Q_LORA) * qn_ref[0]).astype(BF16)
    qf = jnp.dot(qn, wuq_ref[0], preferred_element_type=F32)
    qp = jnp.dot(qn, wrot_ref[0], preferred_element_type=F32)
    cos, sin = cos_ref[...], sin_ref[...]
    cos_all = jnp.concatenate([cos] * B_HEADS, axis=1)
    sin_all = jnp.concatenate([sin] * B_HEADS, axis=1)
    scale = _segment_rms_scale(qf, sel_ref, selt_ref, ilen_ref)
    q_ref[0] = (scale * (qf * gq_ref[0] * cos_all + qp * gqr_ref[0] * sin_all)).astype(BF16)
    lat_ref[0] = rms(ckv, KV_LORA) * kvn_ref[0]
    krn = rms(kr, QK_ROPE) * gk_ref[0]
    half = QK_ROPE // 2
    lane = lax.broadcasted_iota(jnp.int32, (1, LANES), 1)
    partner = jnp.where(lane < ROPE_LO + half, pltpu.roll(krn, LANES - half, 1), pltpu.roll(krn, half, 1))
    krg_ref[0] = krn * cos + partner * sin


def _mla_q(zq, tabs, wts, layer, tm):
    bv, lv, _ = zq.shape
    row = lambda w: pl.BlockSpec((1, tm, w), lambda b, j: (b, j, 0))
    tab = pl.BlockSpec((tm, LANES), lambda b, j: (j, 0))
    par = lambda r, c: pl.BlockSpec((1, r, c), lambda b, j: (layer, 0, 0))
    const = lambda a: pl.BlockSpec(a.shape, lambda b, j: (0, 0))
    segs = _segment_tables([b for h in range(B_HEADS) for b in
                            ((h * HEAD_PAD, h * HEAD_PAD + QK_NOPE), (h * HEAD_PAD + QK_NOPE, h * HEAD_PAD + QK_DIM))])
    return pl.pallas_call(
        _mla_q_kernel,
        grid=(bv, lv // tm),
        in_specs=[row(ZQ_COLS), tab, tab, par(1, Q_LORA), par(Q_LORA, QK_WIDTH), par(Q_LORA, QK_WIDTH),
                  par(1, QK_WIDTH), par(1, QK_WIDTH), par(1, KV_LORA), par(1, LANES)] + [const(a) for a in segs],
        out_specs=[row(QK_WIDTH), row(KV_LORA), row(LANES)],
        out_shape=[jax.ShapeDtypeStruct((bv, lv, QK_WIDTH), BF16),
                   jax.ShapeDtypeStruct((bv, lv, KV_LORA), F32),
                   jax.ShapeDtypeStruct((bv, lv, LANES), F32)],
        compiler_params=_cparams(("parallel", "parallel")),
        name="mla_q",
    )(zq, tabs[0], tabs[1], wts["q_norm"], wts["w_uq"], wts["w_uq_rot"], wts["g_q"], wts["g_q_rot"],
      wts["kv_norm"], wts["g_kr"], *segs)


def _expand_rows(lat_ref, krg_ref, ex_refs, k_ref, v_ref, src_rows, dst_rows):
    wk_ref, wv_ref, gk_ref, sel_ref, selt_ref, ilen_ref = ex_refs
    lat = lat_ref[0, src_rows, :].astype(BF16)
    kf = jnp.dot(lat, wk_ref[0], preferred_element_type=F32)
    scale = _segment_rms_scale(kf, sel_ref, selt_ref, ilen_ref)
    krg = jnp.concatenate([krg_ref[0, src_rows, :]] * B_HEADS, axis=1)
    k_ref[0, dst_rows, :] = (kf * scale * gk_ref[0] + krg).astype(BF16)
    lane = lax.broadcasted_iota(jnp.int32, (1, QK_WIDTH), 1)
    ones_col = jnp.where(lane % HEAD_PAD == V_DIM, 1.0, 0.0)
    v_ref[0, dst_rows, :] = (jnp.dot(lat, wv_ref[0], preferred_element_type=F32) + ones_col).astype(BF16)


def _expand_operands(wts, layer):
    segs = _segment_tables([(h * HEAD_PAD, h * HEAD_PAD + QK_NOPE) for h in range(B_HEADS)])
    par = lambda r, c: pl.BlockSpec((1, r, c), lambda *g: (layer, 0, 0))
    specs = [par(KV_LORA, QK_WIDTH), par(KV_LORA, QK_WIDTH), par(1, QK_WIDTH)]
    specs += [pl.BlockSpec(a.shape, lambda *g: (0, 0)) for a in segs]
    return specs, (wts["w_uk"], wts["w_uv"], wts["g_kn"], *segs)


def _attn_block(q_ref, k_ref, v_ref, start, tk, bias, m_ref, acc_ref):
    rows = pl.ds(start, tk)

    def scores(h):
        lanes = slice(h * HEAD_PAD, (h + 1) * HEAD_PAD)
        s = _mm_nt(q_ref[0, :, lanes], k_ref[0, rows, lanes])
        return s if bias is None else s + bias

    s_next = scores(0)
    for h in range(B_HEADS):
        s = s_next
        if h + 1 < B_HEADS:
            s_next = scores(h + 1)
        m_old = m_ref[h]
        m_new = jnp.maximum(m_old, jnp.max(s, axis=-1, keepdims=True))
        m_wide = m_new[:, :tk] if tk < LANES else jnp.concatenate([m_new] * (tk // LANES), axis=1)
        p = jnp.exp2(s - m_wide)
        pv = jnp.dot(p.astype(BF16), v_ref[0, rows, h * HEAD_PAD:(h + 1) * HEAD_PAD],
                     preferred_element_type=F32)
        acc_ref[h] = jnp.exp2(m_old - m_new) * acc_ref[h] + pv
        m_ref[h] = m_new


def _attn_finish(gb_ref, o_ref, acc_ref):
    outs = []
    for h in range(B_HEADS):
        acc = acc_ref[h]
        outs.append(acc[:, 0:V_DIM] / acc[:, V_DIM:V_DIM + 1])
    gb = gb_ref[0]
    o_ref[0] = (jnp.concatenate(outs, axis=1) * (gb * _sigmoid(gb))).astype(BF16)


def _attn_causal_kernel(q_ref, lat_ref, krg_ref, gb_ref, bias_ref, *rest, t):
    *ex_refs, o_ref, m_ref, acc_ref, k_ref, v_ref = rest
    qi = pl.program_id(1)

    @pl.when(qi == 0)
    def _():
        def expand(r, carry):
            rows = pl.ds(pl.multiple_of(r * t, t), t)
            _expand_rows(lat_ref, krg_ref, ex_refs, k_ref, v_ref, rows, rows)
            return carry

        lax.fori_loop(0, lat_ref.shape[1] // t, expand, 0)

    m_ref[...] = jnp.full(m_ref.shape, NEG_INF, F32)
    acc_ref[...] = jnp.zeros(acc_ref.shape, F32)
    block = functools.partial(_attn_block, q_ref, k_ref, v_ref, tk=t, m_ref=m_ref, acc_ref=acc_ref)
    block(start=0, bias=bias_ref[jnp.where(qi == 0, 2, 1)])

    def kv_step(c, carry):
        block(start=pl.multiple_of(c * t, t), bias=None)
        return carry

    lax.fori_loop(1, qi, kv_step, 0)

    @pl.when(qi > 0)
    def _():
        block(start=pl.multiple_of(qi * t, t), bias=bias_ref[0])

    _attn_finish(gb_ref, o_ref, acc_ref)


def _attention_causal(q, lat, krg, gb, bias, wts, layer, t):
    bs, lq, _ = q.shape
    whole = lambda w: pl.BlockSpec((1, lq, w), lambda b, i: (b, 0, 0))
    ex_specs, ex_args = _expand_operands(wts, layer)
    return pl.pallas_call(
        functools.partial(_attn_causal_kernel, t=t),
        grid=(bs, lq // t),
        in_specs=[pl.BlockSpec((1, t, QK_WIDTH), lambda b, i: (b, i, 0)),
                  whole(KV_LORA), whole(LANES),
                  pl.BlockSpec((1, t, B_WIDTH), lambda b, i: (b, i, 0)),
                  pl.BlockSpec(bias.shape, lambda b, i: (0, 0, 0), pipeline_mode=pl.Buffered(1))] + ex_specs,
        out_specs=pl.BlockSpec((1, t, B_WIDTH), lambda b, i: (b, i, 0)),
        out_shape=jax.ShapeDtypeStruct((bs, lq, B_WIDTH), BF16),
        scratch_shapes=[pltpu.VMEM((B_HEADS, t, LANES), F32), pltpu.VMEM((B_HEADS, t, HEAD_PAD), F32),
                        pltpu.VMEM((1, lq, QK_WIDTH), BF16), pltpu.VMEM((1, lq, QK_WIDTH), BF16)],
        compiler_params=_cparams(("parallel", "arbitrary")),
        name="attention",
    )(q, lat, krg, gb, bias, *ex_args)


def _prompt_bias(t, n_invisible):
    i = np.arange(t)
    diag = np.where((i[None, :] // CHUNK) <= (i[:, None] // CHUNK), 0.0, NEG_INF)
    first = np.broadcast_to(np.where(i[None, :] >= n_invisible, 0.0, NEG_INF), (t, t))
    return jnp.asarray(np.stack([diag, first, np.minimum(diag, first)]), F32)


def _attn_cached_kernel(q_ref, latp_ref, krgp_ref, latn_ref, krgn_ref, gb_ref, *rest, rows_per_step):
    *ex_refs, o_ref, m_ref, acc_ref, kp_ref, vp_ref, kn_ref, vn_ref = rest
    n_past, n_new = latp_ref.shape[2], latn_ref.shape[1]
    for r in range(n_past // rows_per_step):
        rows = pl.ds(r * rows_per_step, rows_per_step)
        _expand_rows(latp_ref.at[0], krgp_ref.at[0], ex_refs, kp_ref, vp_ref, rows, rows)
    _expand_rows(latn_ref, krgn_ref, ex_refs, kn_ref, vn_ref, pl.ds(0, n_new), pl.ds(0, n_new))
    m_ref[...] = jnp.full(m_ref.shape, NEG_INF, F32)
    acc_ref[...] = jnp.zeros(acc_ref.shape, F32)
    _attn_block(q_ref, kp_ref, vp_ref, 0, n_past, None, m_ref, acc_ref)
    _attn_block(q_ref, kn_ref, vn_ref, 0, n_new, None, m_ref, acc_ref)
    _attn_finish(gb_ref, o_ref, acc_ref)


def _attention_cached(q, lat_past, krg_past, lat_new, krg_new, gb, wts, layer):
    bs, lq, _ = q.shape
    n_past = lat_past.shape[2]
    seq = lambda n, w: pl.BlockSpec((1, n, w), lambda b: (b, 0, 0))
    cached = lambda w: pl.BlockSpec((1, 1, n_past, w), lambda b: (layer, b, 0, 0))
    ex_specs, ex_args = _expand_operands(wts, layer)
    return pl.pallas_call(
        functools.partial(_attn_cached_kernel, rows_per_step=_largest_tile(n_past, 512, LANES)),
        grid=(bs,),
        in_specs=[seq(lq, QK_WIDTH), cached(KV_LORA), cached(LANES), seq(lq, KV_LORA), seq(lq, LANES),
                  seq(lq, B_WIDTH)] + ex_specs,
        out_specs=seq(lq, B_WIDTH),
        out_shape=jax.ShapeDtypeStruct((bs, lq, B_WIDTH), BF16),
        scratch_shapes=[pltpu.VMEM((B_HEADS, lq, LANES), F32), pltpu.VMEM((B_HEADS, lq, HEAD_PAD), F32),
                        pltpu.VMEM((1, n_past, QK_WIDTH), BF16), pltpu.VMEM((1, n_past, QK_WIDTH), BF16),
                        pltpu.VMEM((1, lq, QK_WIDTH), BF16), pltpu.VMEM((1, lq, QK_WIDTH), BF16)],
        compiler_params=_cparams(("parallel",)),
        name="attention_cached",
    )(q, lat_past, krg_past, lat_new, krg_new, gb, *ex_args)


def _out_proj_kernel(x_ref, ya_ref, yb_ref, zg_ref, wa_ref, wb_ref, wo_ref, o_ref):
    zg = zg_ref[0]
    ga = _sigmoid(zg[:, 0:D_MODEL])
    gb = _sigmoid(zg[:, D_MODEL:G_COLS])
    t = (ga * jnp.dot(ya_ref[0], wa_ref[0], preferred_element_type=F32)
         + gb * jnp.dot(yb_ref[0], wb_ref[0], preferred_element_type=F32))
    o_ref[0] = x_ref[0] + _mm(t, wo_ref[0])


def _out_proj(x, ya, yb, zg, wts, layer, tm):
    bv, lv, _ = x.shape
    row = lambda w: pl.BlockSpec((1, tm, w), lambda b, j: (b, j, 0))
    par = lambda r, c: pl.BlockSpec((1, r, c), lambda b, j: (layer, 0, 0))
    return pl.pallas_call(
        _out_proj_kernel,
        grid=(bv, lv // tm),
        in_specs=[row(D_MODEL), row(A_WIDTH), row(B_WIDTH), row(G_COLS),
                  par(A_WIDTH, D_MODEL), par(B_WIDTH, D_MODEL), par(D_MODEL, D_MODEL)],
        out_specs=row(D_MODEL),
        out_shape=jax.ShapeDtypeStruct((bv, lv, D_MODEL), F32),
        compiler_params=_cparams(("parallel", "parallel")),
        name="out_proj",
    )(x, ya, yb, zg, wts["w_a"], wts["w_b"], wts["w_o"])


def _prep_weights(w_in, shift_mix, rwkv_w0, rwkv_w2, rwkv_a0, rwkv_a2, rwkv_k_k, rwkv_k_a, rwkv_r_k,
                  rwkv_ln_w, rwkv_ln_b, mla_q_norm, mla_w_uq, mla_kv_norm, mla_w_ukv, mla_qn_nope,
                  mla_kn_nope, mla_qn_rope, mla_kn_rope, w_branch_a, w_branch_b, w_out):
    depth = w_in.shape[0]
    a_end, b0 = A_COLS, A_COLS
    qc = w_in[:, :, b0:b0 + Q_LORA + KV_LORA]
    kr = w_in[:, :, b0 + Q_LORA + KV_LORA:b0 + Q_LORA + KV_LORA + QK_ROPE]
    gate_b = w_in[:, :, b0 + Q_LORA + KV_LORA + QK_ROPE:b0 + B_COLS]
    kr_grp = jnp.pad(kr, ((0, 0), (0, 0), (ROPE_LO, LANES - ROPE_LO - QK_ROPE)))
    w_in_p = jnp.concatenate([w_in[:, :, :a_end], qc, kr_grp, gate_b, w_in[:, :, A_COLS + B_COLS:]],
                             axis=-1).astype(BF16)
    zeros = jnp.zeros((depth, DECAY_LORA, A_WIDTH), F32)
    lora = jnp.concatenate([jnp.concatenate([rwkv_w2, zeros], axis=2),
                            jnp.concatenate([zeros, rwkv_a2], axis=2)], axis=1).astype(BF16)
    row = lambda p: p.reshape(depth, 1, -1)
    w_uq = mla_w_uq.reshape(depth, Q_LORA, B_HEADS, QK_DIM)
    w_uq = jnp.pad(w_uq, ((0, 0), (0, 0), (0, 0), (0, HEAD_PAD - QK_DIM)))
    g_q = jnp.pad(jnp.concatenate([mla_qn_nope, mla_qn_rope], axis=1), ((0, 0), (0, HEAD_PAD - QK_DIM)))
    half = QK_ROPE // 2

    def partner(a):
        lo, hi = a[..., ROPE_LO:ROPE_LO + half], a[..., ROPE_LO + half:ROPE_LO + QK_ROPE]
        zero = jnp.zeros_like(a[..., :ROPE_LO])
        return jnp.concatenate([zero, hi, lo, zero[..., :HEAD_PAD - ROPE_LO - QK_ROPE]], axis=-1)

    q_scale = QK_DIM ** -0.5 * float(np.log2(np.e))
    heads = lambda g: jnp.tile(g, (1, B_HEADS))
    w_uq_rot = partner(w_uq).reshape(depth, Q_LORA, QK_WIDTH)
    w_uq = w_uq.reshape(depth, Q_LORA, QK_WIDTH)
    g_q_rot = heads(partner(g_q)) * q_scale
    g_q = heads(g_q) * q_scale
    g_kr = jnp.pad(mla_kn_rope, ((0, 0), (ROPE_LO, LANES - ROPE_LO - QK_ROPE)))
    g_kn = heads(jnp.pad(mla_kn_nope, ((0, 0), (0, HEAD_PAD - QK_NOPE))))
    w_ukv = mla_w_ukv.reshape(depth, KV_LORA, B_HEADS, QK_NOPE + V_DIM)
    w_uk = jnp.pad(w_ukv[..., :QK_NOPE], ((0, 0), (0, 0), (0, 0), (0, HEAD_PAD - QK_NOPE)))
    w_uv = jnp.pad(w_ukv[..., QK_NOPE:], ((0, 0), (0, 0), (0, 0), (0, HEAD_PAD - V_DIM)))
    return dict(
        w_in=w_in_p, shift_mix=shift_mix, w0=row(rwkv_w0), a0=row(rwkv_a0), lora=lora,
        k_k=row(rwkv_k_k), k_a=row(rwkv_k_a), r_k=row(rwkv_r_k), ln_w=row(rwkv_ln_w), ln_b=row(rwkv_ln_b),
        q_norm=row(mla_q_norm), w_uq=w_uq.astype(BF16), w_uq_rot=w_uq_rot.astype(BF16), g_q=row(g_q),
        g_q_rot=row(g_q_rot), kv_norm=row(mla_kv_norm),
        g_kr=row(g_kr), g_kn=row(g_kn),
        w_uk=w_uk.reshape(depth, KV_LORA, QK_WIDTH).astype(BF16),
        w_uv=w_uv.reshape(depth, KV_LORA, QK_WIDTH).astype(BF16),
        w_a=w_branch_a.astype(BF16), w_b=w_branch_b.astype(BF16), w_o=w_out.astype(BF16))


def _rope_tables(t):
    half = QK_ROPE // 2
    inv = ROPE_THETA ** (-jnp.arange(0, QK_ROPE, 2, dtype=F32) / QK_ROPE)
    ang = t.astype(F32)[:, None] * inv[None, :]
    cos, sin = jnp.cos(ang), jnp.sin(ang)
    n = t.shape[0]
    ones = lambda w: jnp.ones((n, w), F32)
    zeros = lambda w: jnp.zeros((n, w), F32)
    tail = LANES - ROPE_LO - QK_ROPE
    cos_t = jnp.concatenate([ones(ROPE_LO), cos, cos, ones(tail)], axis=1)
    sin_t = jnp.concatenate([zeros(ROPE_LO), -sin, sin, zeros(tail)], axis=1)
    return cos_t, sin_t


def _largest_tile(n, cap, mult):
    assert n % mult == 0, (n, mult)
    best = mult
    for t in range(mult, min(n, cap) + 1, mult):
        if n % t == 0:
            best = t
    return best


def _layer(x, layer, wts, tabs, cfg, shift_row, s0, past=None):
    bv, lv, _ = x.shape
    bs, ls = cfg["bs"], cfg["ls"]
    seq = lambda a: a.reshape(bs, ls, a.shape[-1])
    za, zq, zgb, zg = _in_proj(x, wts["norm_w"], wts["w_in"], layer, cfg["tm"], cfg["n_pad"])
    za_s = seq(za)
    prep = _rwkv_prep(za_s, shift_row, wts, layer, cfg["tm_prep"], cfg["chunk"])
    ya, s_fin = _rwkv_chain(prep, s0, wts, layer, cfg["nb"], cfg["nc"], cfg["chunk"], cfg["g"])
    q, lat, krg = _mla_q(zq, tabs, wts, layer, cfg["tm"])
    if past is None:
        yb = _attention_causal(seq(q), seq(lat), seq(krg), seq(zgb), cfg["bias"], wts, layer, cfg["tq"])
    else:
        yb = _attention_cached(seq(q), past[0], past[1], seq(lat), seq(krg), seq(zgb), wts, layer)
    x_new = _out_proj(x, ya.reshape(bv, lv, A_WIDTH), yb.reshape(bv, lv, B_WIDTH), zg, wts, layer, cfg["tm"])
    return x_new, s_fin, za_s[:, -1:, :], seq(lat), seq(krg)[:, :, ROPE_LO:ROPE_LO + QK_ROPE]


def kernel(x_prompt, x_sample, state_rwkv, state_shift, cache_mla_latent, cache_mla_krope, meta_tokens, norm_w, w_in, shift_mix, rwkv_w0, rwkv_w2, rwkv_a0, rwkv_a2, rwkv_k_k, rwkv_k_a, rwkv_r_k, rwkv_ln_w, rwkv_ln_b, mla_q_norm, mla_w_uq, mla_kv_norm, mla_w_ukv, mla_qn_nope, mla_kn_nope, mla_qn_rope, mla_kn_rope, w_branch_a, w_branch_b, w_out):
    depth = w_in.shape[0]
    bp, seq_len, _ = x_prompt.shape
    bd, dec_len, _ = x_sample.shape
    past_len = cache_mla_latent.shape[2]
    assert seq_len % LANES == 0 and dec_len % 8 == 0
    wts = _prep_weights(w_in, shift_mix, rwkv_w0, rwkv_w2, rwkv_a0, rwkv_a2, rwkv_k_k, rwkv_k_a, rwkv_r_k,
                        rwkv_ln_w, rwkv_ln_b, mla_q_norm, mla_w_uq, mla_kv_norm, mla_w_ukv, mla_qn_nope,
                        mla_kn_nope, mla_qn_rope, mla_kn_rope, w_branch_a, w_branch_b, w_out)
    wts["norm_w"] = norm_w.reshape(depth, 1, D_MODEL)

    n_zero = PAD_FRONT - N_META
    lp = PAD_FRONT + seq_len
    meta = jnp.broadcast_to(meta_tokens[None].astype(F32), (bp, N_META, D_MODEL))
    xp = jnp.concatenate([jnp.zeros((bp, n_zero, D_MODEL), F32), meta, x_prompt], axis=1)
    tabs_p = _rope_tables(jnp.arange(lp, dtype=jnp.int32) - PAD_FRONT)
    tile_p = _largest_tile(lp, 512, LANES)
    cfg_p = dict(bs=bp, ls=lp, tm=tile_p, n_pad=n_zero, tm_prep=_largest_tile(lp, 256, CHUNK),
                 chunk=CHUNK, g=4, nb=4, nc=1, tq=tile_p, bias=_prompt_bias(tile_p, n_zero))
    zero_row = jnp.zeros((bp, 1, A_COLS), F32)
    zero_state = jnp.zeros((bp, A_HEADS, A_HEAD_DIM, A_HEAD_DIM), F32)

    n_rows = bd * dec_len
    xs = x_sample.reshape(1, n_rows, D_MODEL)
    t_s = past_len + (jnp.arange(n_rows, dtype=jnp.int32) % dec_len)
    tabs_s = _rope_tables(t_s)
    cfg_s = dict(bs=bd, ls=dec_len, tm=_largest_tile(n_rows, 512, 8), n_pad=0, tm_prep=dec_len,
                 chunk=dec_len, g=A_HEADS, nb=8, nc=1, tq=dec_len)

    past_lat = cache_mla_latent
    past_krg = jnp.pad(cache_mla_krope, ((0, 0), (0, 0), (0, 0), (ROPE_LO, LANES - ROPE_LO - QK_ROPE)))

    outs_p, outs_s = [], []
    for l in range(depth):
        xp, *op = _layer(xp, l, wts, tabs_p, cfg_p, zero_row, zero_state)
        xs, *os_ = _layer(xs, l, wts, tabs_s, cfg_s, state_shift[l], state_rwkv[l], past=(past_lat, past_krg))
        outs_p.append(op)
        outs_s.append(os_)
    first = PAD_FRONT - N_META
    stack = lambda outs, i, lo=0: jnp.stack([o[i][:, lo:] for o in outs])
    return (xp[:, PAD_FRONT:], xs.reshape(bd, dec_len, D_MODEL),
            stack(outs_p, 0), stack(outs_p, 1), stack(outs_p, 2, first), stack(outs_p, 3, first),
            stack(outs_s, 0), stack(outs_s, 1), stack(outs_s, 2), stack(outs_s, 3))
```
